```python
import math
import jax, jax.numpy as jnp
from jax import lax
import numpy as np

D_MODEL = 1024
BATCH = 16
SEQ = 2048
DEPTH = 2

GDN_HEADS = 4
GDN_HEAD_DIM = 128
GDN_WIDTH = GDN_HEADS * GDN_HEAD_DIM
GDN_CONV = 4
GDN_CHUNK = 64
S5_GROUP = 16
S5_GROUPS = 32
S5_WIDTH = S5_GROUPS * S5_GROUP
S5_STATE = 64
DIFF_HEADS = 4
DIFF_HEAD_DIM = 64
DIFF_V_DIM = 2 * DIFF_HEAD_DIM
DIFF_WIDTH = DIFF_HEADS * DIFF_V_DIM
Q_BLOCK = 128
MIX_WIDTH = GDN_WIDTH + S5_WIDTH + DIFF_WIDTH
IN_SIZES = (GDN_WIDTH, GDN_WIDTH, GDN_WIDTH, GDN_WIDTH, GDN_HEADS, GDN_HEADS,
            S5_WIDTH,
            DIFF_HEADS * 2 * DIFF_HEAD_DIM, DIFF_HEADS * 2 * DIFF_HEAD_DIM, DIFF_WIDTH)
IN_WIDTH = sum(IN_SIZES)
D_FF = 2816
FFN_CONV = 3
EPS = 1e-6

kernel_name = 'hymba_style_gdn_s5_diffattn_convffn'


def rms_norm(x, w, eps=EPS):
    xf = x.astype(jnp.float32)
    y = xf * lax.rsqrt(jnp.mean(xf * xf, axis=-1, keepdims=True) + eps)
    return y.astype(x.dtype) * w


def l2norm(x, eps=1e-6):
    xf = x.astype(jnp.float32)
    return xf * lax.rsqrt(jnp.sum(xf * xf, axis=-1, keepdims=True) + eps)


def split_cols(t, sizes):
    offs = np.cumsum(sizes)[:-1].tolist()
    return jnp.split(t, offs, axis=-1)


def causal_dwconv(x, w):
    k = w.shape[0]
    return lax.conv_general_dilated(
        x, w[:, None, :].astype(x.dtype), window_strides=(1,), padding=[(k - 1, 0)],
        dimension_numbers=('NWC', 'WIO', 'NWC'), feature_group_count=x.shape[-1])


def _gdn_chunk_step(state, inp):
    q_n, k_n, u_n, w_n, g_n, attn_n = inp
    v_new = u_n - jnp.einsum('bhck,bhkv->bhcv', w_n, state)
    out = (jnp.einsum('bhck,bhkv->bhcv', q_n * jnp.exp(g_n)[..., None], state)
           + jnp.einsum('bhcs,bhsv->bhcv', attn_n, v_new))
    g_last = g_n[..., -1]
    state = (state * jnp.exp(g_last)[..., None, None]
             + jnp.einsum('bhck,bhcv->bhkv', k_n * jnp.exp(g_last[..., None] - g_n)[..., None], v_new))
    return state, out


def gated_delta_chunked(q, k, v, beta, g):
    b, l, h, dk = q.shape
    dv = v.shape[-1]
    c = GDN_CHUNK
    n = l // c

    def to_chunks(t):
        return t.reshape(b, n, c, h, -1).transpose(0, 3, 1, 2, 4)

    q = to_chunks(q * dk ** -0.5)
    k = to_chunks(k)
    v = to_chunks(v)
    beta = to_chunks(beta[..., None])[..., 0]
    gc = jnp.cumsum(to_chunks(g[..., None])[..., 0], axis=-1)
    incl = jnp.tril(jnp.ones((c, c), dtype=bool))
    strict = jnp.tril(jnp.ones((c, c), dtype=bool), -1)
    diff = gc[..., :, None] - gc[..., None, :]
    decay = jnp.where(incl, jnp.exp(jnp.where(incl, diff, 0.0)), 0.0)
    k_beta = k * beta[..., None]
    lower = jnp.where(strict, jnp.einsum('bhnck,bhnsk->bhncs', k_beta, k) * decay, 0.0)
    eye = jnp.eye(c, dtype=jnp.float32)
    t_inv = lax.linalg.triangular_solve(lower + eye, jnp.broadcast_to(eye, lower.shape),
                                        left_side=True, lower=True, unit_diagonal=True)
    u = t_inv @ (v * beta[..., None])
    w = t_inv @ (k_beta * jnp.exp(gc)[..., None])
    attn = jnp.where(incl, jnp.einsum('bhnck,bhnsk->bhncs', q, k) * decay, 0.0)
    xs = tuple(jnp.moveaxis(t, 2, 0) for t in (q, k, u, w, gc, attn))
    s0 = jnp.zeros((b, h, dk, dv), jnp.float32)
    _, out = lax.scan(_gdn_chunk_step, s0, xs)
    return out.transpose(1, 0, 3, 2, 4).reshape(b, l, h, dv)


def _linear_recurrence_combine(left, right):
    a_i, b_i = left
    a_j, b_j = right
    return a_j * a_i, a_j * b_i + b_j


def s5_mixer(u, lam_re, lam_im, log_dt, b_re, b_im, c_re, c_im, d_skip, w_glu, b_glu):
    f32 = jnp.float32
    b, l, _ = u.shape
    uf = u.astype(f32).reshape(b, l, S5_GROUPS, S5_GROUP)
    lam = lax.complex(jnp.minimum(lam_re.astype(f32), -1e-4), lam_im.astype(f32))
    dt = jnp.exp(log_dt.astype(f32))[:, None]
    lam_bar = jnp.exp(lam * dt)
    b_bar = ((lam_bar - 1.0) / lam)[..., None] * lax.complex(b_re.astype(f32), b_im.astype(f32))
    bu = jnp.einsum('gph,blgh->blgp', b_bar, uf.astype(jnp.complex64))
    a = jnp.broadcast_to(lam_bar, (1, l) + lam_bar.shape)
    _, states = lax.associative_scan(_linear_recurrence_combine, (a, bu), axis=1)
    c = lax.complex(c_re.astype(f32), c_im.astype(f32))
    y = (jnp.real(jnp.einsum('ghp,blgp->blgh', c, states))
         + d_skip.astype(f32).reshape(S5_GROUPS, S5_GROUP) * uf)
    y = jax.nn.gelu(y.reshape(b, l, S5_WIDTH))
    y = y * jax.nn.sigmoid(y @ w_glu.astype(f32) + b_glu.astype(f32))
    return y.astype(u.dtype)


def diff_attention(q, k, v, lam):
    b, l, h, _, dh = q.shape
    nb = l // Q_BLOCK
    qf = (q.astype(jnp.float32) * dh ** -0.5).reshape(b, nb, Q_BLOCK, h, 2, dh).swapaxes(0, 1)
    kf = k.astype(jnp.float32)
    vf = v.astype(jnp.float32)
    key_pos = jnp.arange(l)

    def block(args):
        q_blk, start = args
        s = jnp.einsum('bqhcd,bkhcd->bhcqk', q_blk, kf)
        causal = (start + jnp.arange(Q_BLOCK))[:, None] >= key_pos[None, :]
        p = jax.nn.softmax(jnp.where(causal, s, -jnp.inf), axis=-1)
        a = p[:, :, 0] - lam * p[:, :, 1]
        return jnp.einsum('bhqk,bkhe->bqhe', a, vf)

    starts = jnp.arange(nb, dtype=jnp.int32) * Q_BLOCK
    out = lax.map(block, (qf, starts))
    return out.swapaxes(0, 1).reshape(b, l, h, -1).astype(v.dtype)


def conv_glu_ffn(h, w_up, conv_w, conv_b, w_down):
    u = causal_dwconv(h @ w_up, conv_w) + conv_b
    gate, up = jnp.split(u, 2, axis=-1)
    return (jax.nn.silu(gate) * up) @ w_down


def setup_inputs(seed: int = 0) -> dict:
    key = jax.random.key(seed)
    ks = iter(jax.random.split(key, 40))
    f32 = jnp.float32

    def nrm(shape, scale):
        return jax.random.normal(next(ks), shape, f32) * scale

    def gain(shape):
        return 1.0 + 0.01 * jax.random.normal(next(ks), shape, f32)

    x = nrm((BATCH, SEQ, D_MODEL), 1.0)
    attn_norm_w = gain((DEPTH, D_MODEL))
    w_in = nrm((DEPTH, D_MODEL, IN_WIDTH), D_MODEL ** -0.5)
    gdn_conv_w = nrm((DEPTH, GDN_CONV, 3 * GDN_WIDTH), GDN_CONV ** -0.5)
    gdn_a_log = jnp.log(jax.random.uniform(next(ks), (DEPTH, GDN_HEADS), f32, 1.0, 16.0))
    gdn_dt = jnp.exp(jax.random.uniform(next(ks), (DEPTH, GDN_HEADS), f32, math.log(1e-3), math.log(1e-1)))
    gdn_dt_bias = gdn_dt + jnp.log(-jnp.expm1(-gdn_dt))
    gdn_norm_w = gain((DEPTH, GDN_HEAD_DIM))
    s5_lambda_re = -0.5 + nrm((DEPTH, S5_GROUPS, S5_STATE), 0.01)
    s5_lambda_im = jnp.pi * jnp.arange(S5_STATE, dtype=f32) + nrm((DEPTH, S5_GROUPS, S5_STATE), 0.01)
    s5_log_dt = jax.random.uniform(next(ks), (DEPTH, S5_GROUPS), f32, math.log(1e-3), math.log(1e-1))
    s5_b_re = nrm((DEPTH, S5_GROUPS, S5_STATE, S5_GROUP), (2 * S5_GROUP) ** -0.5)
    s5_b_im = nrm((DEPTH, S5_GROUPS, S5_STATE, S5_GROUP), (2 * S5_GROUP) ** -0.5)
    s5_c_re = nrm((DEPTH, S5_GROUPS, S5_GROUP, S5_STATE), S5_STATE ** -0.5)
    s5_c_im = nrm((DEPTH, S5_GROUPS, S5_GROUP, S5_STATE), S5_STATE ** -0.5)
    s5_d = nrm((DEPTH, S5_WIDTH), 1.0)
    s5_w_glu = nrm((DEPTH, S5_WIDTH, S5_WIDTH), S5_WIDTH ** -0.5)
    s5_b_glu = nrm((DEPTH, S5_WIDTH), 0.01)
    s5_norm_w = gain((DEPTH, S5_WIDTH))
    diff_lambda_q1 = nrm((DEPTH, DIFF_HEAD_DIM), 0.1)
    diff_lambda_k1 = nrm((DEPTH, DIFF_HEAD_DIM), 0.1)
    diff_lambda_q2 = nrm((DEPTH, DIFF_HEAD_DIM), 0.1)
    diff_lambda_k2 = nrm((DEPTH, DIFF_HEAD_DIM), 0.1)
    diff_norm_w = gain((DEPTH, DIFF_V_DIM))
    w_out = nrm((DEPTH, MIX_WIDTH, D_MODEL), MIX_WIDTH ** -0.5)
    ffn_norm_w = gain((DEPTH, D_MODEL))
    ffn_w_up = nrm((DEPTH, D_MODEL, 2 * D_FF), D_MODEL ** -0.5)
    ffn_conv_w = nrm((DEPTH, FFN_CONV, 2 * D_FF), FFN_CONV ** -0.5)
    ffn_conv_b = nrm((DEPTH, 2 * D_FF), 0.01)
    ffn_w_down = nrm((DEPTH, D_FF, D_MODEL), D_FF ** -0.5)
    final_norm_w = gain((D_MODEL,))
    return {'x': x, 'attn_norm_w': attn_norm_w, 'w_in': w_in, 'gdn_conv_w': gdn_conv_w,
            'gdn_a_log': gdn_a_log, 'gdn_dt_bias': gdn_dt_bias, 'gdn_norm_w': gdn_norm_w,
            's5_lambda_re': s5_lambda_re, 's5_lambda_im': s5_lambda_im, 's5_log_dt': s5_log_dt,
            's5_b_re': s5_b_re, 's5_b_im': s5_b_im, 's5_c_re': s5_c_re, 's5_c_im': s5_c_im,
            's5_d': s5_d, 's5_w_glu': s5_w_glu, 's5_b_glu': s5_b_glu, 's5_norm_w': s5_norm_w,
            'diff_lambda_q1': diff_lambda_q1, 'diff_lambda_k1': diff_lambda_k1,
            'diff_lambda_q2': diff_lambda_q2, 'diff_lambda_k2': diff_lambda_k2,
            'diff_norm_w': diff_norm_w, 'w_out': w_out, 'ffn_norm_w': ffn_norm_w,
            'ffn_w_up': ffn_w_up, 'ffn_conv_w': ffn_conv_w, 'ffn_conv_b': ffn_conv_b,
            'ffn_w_down': ffn_w_down, 'final_norm_w': final_norm_w}


def reference(x, attn_norm_w, w_in, gdn_conv_w, gdn_a_log, gdn_dt_bias, gdn_norm_w,
              s5_lambda_re, s5_lambda_im, s5_log_dt, s5_b_re, s5_b_im, s5_c_re, s5_c_im,
              s5_d, s5_w_glu, s5_b_glu, s5_norm_w, diff_lambda_q1, diff_lambda_k1,
              diff_lambda_q2, diff_lambda_k2, diff_norm_w, w_out, ffn_norm_w, ffn_w_up,
              ffn_conv_w, ffn_conv_b, ffn_w_down, final_norm_w):
    f32 = jnp.float32
    b, l, _ = x.shape
    for i in range(DEPTH):
        h = rms_norm(x, attn_norm_w[i])
        g_q, g_k, g_v, g_z, g_b, g_a, s_u, d_q, d_k, d_v = split_cols(h @ w_in[i], IN_SIZES)

        qkv = jax.nn.silu(causal_dwconv(jnp.concatenate([g_q, g_k, g_v], axis=-1), gdn_conv_w[i]))
        g_q, g_k, g_v = (t.reshape(b, l, GDN_HEADS, GDN_HEAD_DIM) for t in jnp.split(qkv, 3, axis=-1))
        beta = jax.nn.sigmoid(g_b.astype(f32))
        log_decay = (-jnp.exp(gdn_a_log[i].astype(f32))
                     * jax.nn.softplus(g_a.astype(f32) + gdn_dt_bias[i].astype(f32)))
        o_gdn = gated_delta_chunked(l2norm(g_q), l2norm(g_k), g_v.astype(f32), beta, log_decay)
        o_gdn = (rms_norm(o_gdn, gdn_norm_w[i].astype(f32))
                 * jax.nn.silu(g_z.astype(f32).reshape(b, l, GDN_HEADS, GDN_HEAD_DIM)))
        o_gdn = o_gdn.reshape(b, l, GDN_WIDTH).astype(x.dtype)

        o_s5 = rms_norm(s5_mixer(s_u, s5_lambda_re[i], s5_lambda_im[i], s5_log_dt[i], s5_b_re[i],
                                 s5_b_im[i], s5_c_re[i], s5_c_im[i], s5_d[i], s5_w_glu[i], s5_b_glu[i]),
                        s5_norm_w[i])

        lam_init = 0.8 - 0.6 * math.exp(-0.3 * i)
        lam = (jnp.exp(jnp.sum(diff_lambda_q1[i].astype(f32) * diff_lambda_k1[i].astype(f32)))
               - jnp.exp(jnp.sum(diff_lambda_q2[i].astype(f32) * diff_lambda_k2[i].astype(f32)))
               + lam_init)
        o_diff = diff_attention(d_q.reshape(b, l, DIFF_HEADS, 2, DIFF_HEAD_DIM),
                                d_k.reshape(b, l, DIFF_HEADS, 2, DIFF_HEAD_DIM),
                                d_v.reshape(b, l, DIFF_HEADS, DIFF_V_DIM), lam)
        o_diff = (rms_norm(o_diff, diff_norm_w[i]) * (1.0 - lam_init)).reshape(b, l, DIFF_WIDTH)

        x = x + jnp.concatenate([o_gdn, o_s5, o_diff], axis=-1) @ w_out[i]

        x = x + conv_glu_ffn(rms_norm(x, ffn_norm_w[i]), ffn_w_up[i], ffn_conv_w[i],
                             ffn_conv_b[i], ffn_w_down[i])
    return rms_norm(x, final_norm_w)
```

```python
import functools
import math

import jax
import jax.numpy as jnp
from jax import lax
from jax.experimental import pallas as pl
from jax.experimental.pallas import tpu as pltpu

F32 = jnp.float32
BF16 = jnp.bfloat16
HIGHEST = lax.Precision.HIGHEST

D_MODEL = 1024
EPS = 1e-6
GDN_HEADS = 4
GDN_DIM = 128
GDN_WIDTH = GDN_HEADS * GDN_DIM
GDN_CONV = 4
GDN_CHUNK = 64
GDN_INV_BLOCK = 16
S5_GROUP = 16
S5_GROUPS = 32
S5_WIDTH = S5_GROUPS * S5_GROUP
S5_STATE = 64
S5_NSTATE = S5_GROUPS * S5_STATE
S5_HALF_IN = S5_WIDTH // 2
S5_HALF_ST = S5_NSTATE // 2
DIFF_HEADS = 4
DIFF_DIM = 64
DIFF_V = 2 * DIFF_DIM
DIFF_WIDTH = DIFF_HEADS * DIFF_V
MIX_WIDTH = GDN_WIDTH + S5_WIDTH + DIFF_WIDTH
D_FF = 2816
FFN_CONV = 3
FFN_CHUNK = 256
FFN_NCHUNK = D_FF // FFN_CHUNK

LANES = 128
SUBLANES = 8
VMEM_LIMIT = 56 * 1024 * 1024

INF_QKV = 3 * GDN_WIDTH
INF_WIDTH = INF_QKV + GDN_WIDTH + LANES


def _dot(a, b):
    return jnp.dot(a, b, preferred_element_type=F32)


def _dot_nt(a, b, precision=None):
    return lax.dot_general(a, b, (((1,), (1,)), ((), ())), precision=precision,
                           preferred_element_type=F32)


def _dot_tn(a, b):
    return lax.dot_general(a, b, (((0,), (0,)), ((), ())), preferred_element_type=F32)


def _sigmoid(x):
    return 1.0 / (1.0 + jnp.exp(-x))


def _silu(x):
    return x * _sigmoid(x)


def _params(sem):
    return pltpu.CompilerParams(dimension_semantics=sem, vmem_limit_bytes=VMEM_LIMIT)


def _inproj_kernel(x_ref, nw_ref, wf_ref, ws_ref, wa_ref, of_ref, os_ref, oa_ref):
    x = x_ref[...]
    ms = jnp.mean(x * x, axis=-1, keepdims=True)
    h = ((x * lax.rsqrt(ms + EPS)) * nw_ref[...]).astype(BF16)
    of_ref[...] = _dot(h, wf_ref[...])
    os_ref[...] = _dot(h, ws_ref[...])
    oa_ref[...] = _dot(h, wa_ref[...]).astype(BF16)


def _inproj(x2, nw, wf, ws, wa, batch, seq, tm):
    nt = seq // tm
    tokens = batch * seq
    row = lambda b, t: (b * nt + t, 0)
    const = lambda b, t: (0, 0)
    return pl.pallas_call(
        _inproj_kernel,
        grid=(batch, nt),
        in_specs=[pl.BlockSpec((tm, D_MODEL), row),
                  pl.BlockSpec((1, D_MODEL), const),
                  pl.BlockSpec((D_MODEL, INF_WIDTH), const),
                  pl.BlockSpec((D_MODEL, S5_WIDTH), const),
                  pl.BlockSpec((D_MODEL, 3 * DIFF_WIDTH), const)],
        out_specs=[pl.BlockSpec((tm, INF_WIDTH), row),
                   pl.BlockSpec((tm, S5_WIDTH), lambda b, t: (t, b)),
                   pl.BlockSpec((tm, 3 * DIFF_WIDTH), row)],
        out_shape=[jax.ShapeDtypeStruct((tokens, INF_WIDTH), F32),
                   jax.ShapeDtypeStruct((seq, batch * S5_WIDTH), F32),
                   jax.ShapeDtypeStruct((tokens, 3 * DIFF_WIDTH), BF16)],
        compiler_params=_params(("parallel", "parallel")),
        name="inproj",
    )(x2, nw, wf, ws, wa)


def _unit_lower_inverse(a, row, col):
    shift = int(math.log2(GDN_INV_BLOCK))
    same = (row >> shift) == (col >> shift)
    eye = (row == col).astype(F32)
    ad = jnp.where(same, a, 0.0)
    ao = jnp.where(same, 0.0, a)
    mm = lambda p, q: _dot(p.astype(BF16), q.astype(BF16))
    p = eye - ad
    x = mm(ad, ad)
    p = p + mm(p, x)
    x = mm(x, x)
    p = p + mm(p, x)
    x = mm(x, x)
    td = p + mm(p, x)
    n1 = mm(td, ao)
    n2 = mm(n1, n1)
    n3 = mm(n1, n2)
    return mm(eye - n1 + n2 - n3, td)


def _gdn_kernel(qkv_ref, z_ref, ba_ref, cw_ref, gp_ref, nw_ref, o_ref, xs_ref, q_s, k_s, v_s, st_ref,
                *, tl):
    c = GDN_CHUNK
    ti = pl.program_id(1)

    @pl.when(ti == 0)
    def _():
        xs_ref[0:SUBLANES, :] = jnp.zeros((SUBLANES, INF_QKV), F32)
        st_ref[...] = jnp.zeros_like(st_ref)

    xs_ref[SUBLANES:SUBLANES + tl, :] = qkv_ref[...]
    for h in range(GDN_HEADS):
        for part, dst in enumerate((q_s, k_s, v_s)):
            lo = part * GDN_WIDTH + h * GDN_DIM
            acc = jnp.zeros((tl, GDN_DIM), F32)
            for j in range(GDN_CONV):
                off = SUBLANES - (GDN_CONV - 1) + j
                acc = acc + xs_ref[off:off + tl, lo:lo + GDN_DIM] * cw_ref[j:j + 1, lo:lo + GDN_DIM]
            y = _silu(acc)
            if part < 2:
                y = y * lax.rsqrt(jnp.sum(y * y, axis=-1, keepdims=True) + 1e-6)
            if part == 0:
                y = y * (GDN_DIM ** -0.5)
            dst[h] = y
    xs_ref[0:SUBLANES, :] = xs_ref[tl:tl + SUBLANES, :]

    ba = ba_ref[...]
    beta_all = _sigmoid(ba)
    sp = ba + gp_ref[1:2, :]
    softplus = jnp.maximum(sp, 0.0) + jnp.log(1.0 + jnp.exp(-jnp.abs(sp)))
    g_all = -jnp.exp(gp_ref[0:1, :]) * softplus

    row = lax.broadcasted_iota(jnp.int32, (c, c), 0)
    col = lax.broadcasted_iota(jnp.int32, (c, c), 1)
    incl = row >= col
    strict = row > col
    tri = incl.astype(F32)
    sel_r = lax.broadcasted_iota(jnp.int32, (SUBLANES, LANES), 0)
    sel_c = lax.broadcasted_iota(jnp.int32, (SUBLANES, LANES), 1)
    sel = (sel_c == sel_r + GDN_HEADS).astype(F32)

    for ci in range(tl // c):
        rows = slice(ci * c, (ci + 1) * c)
        gcol_all = jnp.dot(tri, g_all[rows, :], precision=HIGHEST, preferred_element_type=F32)
        grow_all = _dot_nt(sel, gcol_all, precision=HIGHEST)
        for h in range(GDN_HEADS):
            q = q_s[h, rows, :]
            k = k_s[h, rows, :]
            v = v_s[h, rows, :]
            beta = beta_all[rows, h:h + 1]
            gc = gcol_all[:, GDN_HEADS + h:GDN_HEADS + h + 1]
            gr = grow_all[h:h + 1, :]
            decay = jnp.where(incl, jnp.exp(jnp.where(incl, gc - gr, 0.0)), 0.0)
            egc = jnp.exp(gc)
            kb = k * beta
            kbf = k.astype(BF16)
            a_low = jnp.where(strict, _dot_nt(kb.astype(BF16), kbf) * decay, 0.0)
            attn = jnp.where(incl, _dot_nt(q.astype(BF16), kbf) * decay, 0.0)
            t_inv = _unit_lower_inverse(a_low, row, col).astype(BF16)
            u = _dot(t_inv, (v * beta).astype(BF16))
            w = _dot(t_inv, (kb * egc).astype(BF16))
            s = st_ref[h]
            sb = s.astype(BF16)
            v_new = u - _dot(w.astype(BF16), sb)
            o = _dot((q * egc).astype(BF16), sb) + _dot(attn.astype(BF16), v_new.astype(BF16))
            g_last = gc[c - 1:c, :]
            kd = k * jnp.exp(g_last - gc)
            st_ref[h] = s * jnp.exp(g_last) + _dot_tn(kd.astype(BF16), v_new.astype(BF16))
            on = o * lax.rsqrt(jnp.mean(o * o, axis=-1, keepdims=True) + EPS) * nw_ref[...]
            zz = z_ref[rows, h * GDN_DIM:(h + 1) * GDN_DIM]
            o_ref[rows, h * GDN_DIM:(h + 1) * GDN_DIM] = on * _silu(zz)


def _gdn(inf, conv_w, gate_p, norm_w, batch, seq, tl):
    nt = seq // tl
    tokens = batch * seq
    row = lambda b, t: (b * nt + t, 0)
    const = lambda b, t: (0, 0)
    return pl.pallas_call(
        functools.partial(_gdn_kernel, tl=tl),
        grid=(batch, nt),
        in_specs=[pl.BlockSpec((tl, INF_QKV), row),
                  pl.BlockSpec((tl, GDN_WIDTH), lambda b, t: (b * nt + t, INF_QKV // GDN_WIDTH)),
                  pl.BlockSpec((tl, LANES), lambda b, t: (b * nt + t, (INF_QKV + GDN_WIDTH) // LANES)),
                  pl.BlockSpec((GDN_CONV, INF_QKV), const),
                  pl.BlockSpec((SUBLANES, LANES), const),
                  pl.BlockSpec((1, GDN_DIM), const)],
        out_specs=pl.BlockSpec((tl, GDN_WIDTH), row),
        out_shape=jax.ShapeDtypeStruct((tokens, GDN_WIDTH), F32),
        scratch_shapes=[pltpu.VMEM((tl + SUBLANES, INF_QKV), F32),
                        pltpu.VMEM((GDN_HEADS, tl, GDN_DIM), F32),
                        pltpu.VMEM((GDN_HEADS, tl, GDN_DIM), F32),
                        pltpu.VMEM((GDN_HEADS, tl, GDN_DIM), F32),
                        pltpu.VMEM((GDN_HEADS, GDN_DIM, GDN_DIM), F32)],
        compiler_params=_params(("parallel", "arbitrary")),
        name="gdn",
    )(inf, inf, inf, conv_w, gate_p, norm_w)


S5_SCAN_LANES = 512


def _gelu_tanh(x):
    return 0.5 * x * (1.0 + jnp.tanh(math.sqrt(2.0 / math.pi) * (x + 0.044715 * (x * x * x))))


def _s5_kernel(u_ref, a_ref, bre_ref, bim_ref, cre_ref, cim_ref, d_ref, wg_ref, bg_ref, nw_ref, o_ref,
               sre_ref, sim_ref, *, batch, tl):
    rows = batch * tl

    @pl.when(pl.program_id(0) == 0)
    def _():
        sre_ref[0:batch, :] = jnp.zeros((batch, S5_NSTATE), F32)
        sim_ref[0:batch, :] = jnp.zeros((batch, S5_NSTATE), F32)

    u = u_ref[...]
    ub = u.astype(BF16)
    for hf in range(2):
        cin = slice(hf * S5_HALF_IN, (hf + 1) * S5_HALF_IN)
        cst = slice(hf * S5_HALF_ST, (hf + 1) * S5_HALF_ST)
        sre_ref[batch:batch + rows, cst] = _dot(ub[:, cin], bre_ref[hf])
        sim_ref[batch:batch + rows, cst] = _dot(ub[:, cin], bim_ref[hf])

    for lc in range(S5_NSTATE // S5_SCAN_LANES):
        cols = slice(lc * S5_SCAN_LANES, (lc + 1) * S5_SCAN_LANES)
        a_re = jnp.broadcast_to(a_ref[0:1, cols], (batch, S5_SCAN_LANES))
        a_im = jnp.broadcast_to(a_ref[1:2, cols], (batch, S5_SCAN_LANES))

        def step(t, carry):
            s_re, s_im = carry
            r = pl.ds(pl.multiple_of((t + 1) * batch, batch), batch)
            n_re = a_re * s_re - a_im * s_im + sre_ref[r, cols]
            n_im = a_re * s_im + a_im * s_re + sim_ref[r, cols]
            sre_ref[r, cols] = n_re
            sim_ref[r, cols] = n_im
            return n_re, n_im

        s_re, s_im = lax.fori_loop(0, tl, step, (sre_ref[0:batch, cols], sim_ref[0:batch, cols]))
        sre_ref[0:batch, cols] = s_re
        sim_ref[0:batch, cols] = s_im

    ys = []
    for hf in range(2):
        cst = slice(hf * S5_HALF_ST, (hf + 1) * S5_HALF_ST)
        s_re = sre_ref[batch:batch + rows, cst].astype(BF16)
        s_im = sim_ref[batch:batch + rows, cst].astype(BF16)
        ys.append(_dot(s_re, cre_ref[hf]) - _dot(s_im, cim_ref[hf]))
    y = jnp.concatenate(ys, axis=-1) + d_ref[...] * u
    y = _gelu_tanh(y)
    y = y * _sigmoid(_dot(y.astype(BF16), wg_ref[...]) + bg_ref[...])
    o_ref[...] = y * lax.rsqrt(jnp.mean(y * y, axis=-1, keepdims=True) + EPS) * nw_ref[...]


def _s5(u_tm, a_bar, bre, bim, cre, cim, d_skip, w_glu, b_glu, norm_w, batch, seq, tl):
    rows = batch * tl
    blk = lambda t: (t, 0)
    c2 = lambda t: (0, 0)
    c3 = lambda t: (0, 0, 0)
    return pl.pallas_call(
        functools.partial(_s5_kernel, batch=batch, tl=tl),
        grid=(seq // tl,),
        in_specs=[pl.BlockSpec((rows, S5_WIDTH), blk),
                  pl.BlockSpec((SUBLANES, S5_NSTATE), c2),
                  pl.BlockSpec((2, S5_HALF_IN, S5_HALF_ST), c3),
                  pl.BlockSpec((2, S5_HALF_IN, S5_HALF_ST), c3),
                  pl.BlockSpec((2, S5_HALF_ST, S5_HALF_IN), c3),
                  pl.BlockSpec((2, S5_HALF_ST, S5_HALF_IN), c3),
                  pl.BlockSpec((1, S5_WIDTH), c2),
                  pl.BlockSpec((S5_WIDTH, S5_WIDTH), c2),
                  pl.BlockSpec((1, S5_WIDTH), c2),
                  pl.BlockSpec((1, S5_WIDTH), c2)],
        out_specs=pl.BlockSpec((rows, S5_WIDTH), blk),
        out_shape=jax.ShapeDtypeStruct((seq * batch, S5_WIDTH), F32),
        scratch_shapes=[pltpu.VMEM((batch + rows, S5_NSTATE), F32),
                        pltpu.VMEM((batch + rows, S5_NSTATE), F32)],
        compiler_params=_params(("arbitrary",)),
        name="s5",
    )(u_tm, a_bar, bre, bim, cre, cim, d_skip, w_glu, b_glu, norm_w)


def _s5_operators(lam_re, lam_im, log_dt, b_re, b_im, c_re, c_im):
    lam = lax.complex(jnp.minimum(lam_re.astype(F32), -1e-4), lam_im.astype(F32))
    dt = jnp.exp(log_dt.astype(F32))[:, None]
    lam_bar = jnp.exp(lam * dt)
    b_bar = ((lam_bar - 1.0) / lam)[..., None] * lax.complex(b_re.astype(F32), b_im.astype(F32))
    a_bar = jnp.zeros((SUBLANES, S5_NSTATE), F32)
    a_bar = a_bar.at[0].set(jnp.real(lam_bar).reshape(-1)).at[1].set(jnp.imag(lam_bar).reshape(-1))
    eye = jnp.eye(S5_GROUPS, dtype=F32)

    def b_op(t):
        full = jnp.einsum('gph,gk->ghkp', t, eye).reshape(S5_WIDTH, S5_NSTATE)
        return jnp.stack([full[:S5_HALF_IN, :S5_HALF_ST], full[S5_HALF_IN:, S5_HALF_ST:]]).astype(BF16)

    def c_op(t):
        full = jnp.einsum('ghp,gk->gpkh', t, eye).reshape(S5_NSTATE, S5_WIDTH)
        return jnp.stack([full[:S5_HALF_ST, :S5_HALF_IN], full[S5_HALF_ST:, S5_HALF_IN:]]).astype(BF16)

    return (a_bar, b_op(jnp.real(b_bar)), b_op(jnp.imag(b_bar)),
            c_op(c_re.astype(F32)), c_op(c_im.astype(F32)))


def _diff_kernel(q_ref, k_ref, v_ref, lp_ref, nw_ref, o_ref, m_ref, l_ref, acc_ref, *, tq, lam_init):
    qi = pl.program_id(2)
    q = q_ref[...]
    lane = lax.broadcasted_iota(jnp.int32, (tq, DIFF_V), 1)
    scale = DIFF_DIM ** -0.5
    zero = jnp.zeros_like(q)
    q0 = jnp.where(lane < DIFF_DIM, q, zero)
    q1 = jnp.where(lane < DIFF_DIM, zero, q)
    qs = (jnp.concatenate([q0, q1], axis=0).astype(F32) * scale).astype(BF16)

    m_ref[...] = jnp.full_like(m_ref, -jnp.inf)
    l_ref[...] = jnp.zeros_like(l_ref)
    acc_ref[...] = jnp.zeros_like(acc_ref)

    def update(kb, masked):
        r = pl.ds(pl.multiple_of(kb * tq, tq), tq)
        s = _dot_nt(qs, k_ref[r, :])
        if masked:
            rr = lax.broadcasted_iota(jnp.int32, (2 * tq, tq), 0)
            cc = lax.broadcasted_iota(jnp.int32, (2 * tq, tq), 1)
            rr = jnp.where(rr >= tq, rr - tq, rr)
            s = jnp.where(rr >= cc, s, -jnp.inf)
        m_prev = m_ref[...]
        m_new = jnp.maximum(m_prev, jnp.max(s, axis=-1, keepdims=True))
        alpha = jnp.exp(m_prev - m_new)
        p = jnp.exp(s - m_new)
        l_ref[...] = alpha * l_ref[...] + jnp.sum(p, axis=-1, keepdims=True)
        acc_ref[...] = alpha * acc_ref[...] + _dot(p.astype(BF16), v_ref[r, :])
        m_ref[...] = m_new

    def body(kb, carry):
        update(kb, False)
        return carry

    lax.fori_loop(0, qi, body, 0)
    update(qi, True)

    lp = lp_ref[...]
    lam = (jnp.exp(jnp.sum(lp[0:1, :] * lp[1:2, :], axis=-1, keepdims=True))
           - jnp.exp(jnp.sum(lp[2:3, :] * lp[3:4, :], axis=-1, keepdims=True)) + lam_init)
    o = acc_ref[...] / l_ref[...]
    o = o[:tq, :] - lam * o[tq:, :]
    on = o * lax.rsqrt(jnp.mean(o * o, axis=-1, keepdims=True) + EPS) * nw_ref[...]
    o_ref[...] = on * (1.0 - lam_init)


def _diff_attn(qkv, lam_p, norm_w, batch, seq, tq, lam_init):
    nq = seq // tq
    tokens = batch * seq
    return pl.pallas_call(
        functools.partial(_diff_kernel, tq=tq, lam_init=lam_init),
        grid=(batch, DIFF_HEADS, nq),
        in_specs=[pl.BlockSpec((tq, DIFF_V), lambda b, h, i: (b * nq + i, h)),
                  pl.BlockSpec((seq, DIFF_V), lambda b, h, i: (b, DIFF_HEADS + h)),
                  pl.BlockSpec((seq, DIFF_V), lambda b, h, i: (b, 2 * DIFF_HEADS + h)),
                  pl.BlockSpec((SUBLANES, LANES), lambda b, h, i: (0, 0)),
                  pl.BlockSpec((1, DIFF_V), lambda b, h, i: (0, 0))],
        out_specs=pl.BlockSpec((tq, DIFF_V), lambda b, h, i: (b * nq + i, h)),
        out_shape=jax.ShapeDtypeStruct((tokens, DIFF_WIDTH), F32),
        scratch_shapes=[pltpu.VMEM((2 * tq, 1), F32),
                        pltpu.VMEM((2 * tq, 1), F32),
                        pltpu.VMEM((2 * tq, DIFF_V), F32)],
        compiler_params=_params(("parallel", "parallel", "parallel")),
        name="diff_attn",
    )(qkv, qkv, qkv, lam_p, norm_w)


def _outproj_kernel(x_ref, og_ref, os_ref, od_ref, w_ref, o_ref):
    acc = _dot(og_ref[...].astype(BF16), w_ref[0:GDN_WIDTH, :])
    acc = acc + _dot(os_ref[...].astype(BF16), w_ref[GDN_WIDTH:GDN_WIDTH + S5_WIDTH, :])
    acc = acc + _dot(od_ref[...].astype(BF16), w_ref[GDN_WIDTH + S5_WIDTH:MIX_WIDTH, :])
    o_ref[...] = x_ref[...] + acc


def _outproj(x2, o_gdn, o_s5_tm, o_diff, w_out, batch, seq, tm):
    nt = seq // tm
    row = lambda b, t: (b * nt + t, 0)
    return pl.pallas_call(
        _outproj_kernel,
        grid=(batch, nt),
        in_specs=[pl.BlockSpec((tm, D_MODEL), row),
                  pl.BlockSpec((tm, GDN_WIDTH), row),
                  pl.BlockSpec((tm, S5_WIDTH), lambda b, t: (t, b)),
                  pl.BlockSpec((tm, DIFF_WIDTH), row),
                  pl.BlockSpec((MIX_WIDTH, D_MODEL), lambda b, t: (0, 0))],
        out_specs=pl.BlockSpec((tm, D_MODEL), row),
        out_shape=jax.ShapeDtypeStruct(x2.shape, F32),
        compiler_params=_params(("parallel", "parallel")),
        name="outproj",
    )(x2, o_gdn, o_s5_tm, o_diff, w_out)


def _ffn_kernel(x_ref, nw_ref, wu_ref, cw_ref, cb_ref, wd_ref, fw_ref, o_ref, ub_ref, tail_ref,
                *, tm, final_norm):
    @pl.when(pl.program_id(1) == 0)
    def _():
        tail_ref[...] = jnp.zeros_like(tail_ref)

    x = x_ref[...]
    ms = jnp.mean(x * x, axis=-1, keepdims=True)
    h = ((x * lax.rsqrt(ms + EPS)) * nw_ref[...]).astype(BF16)
    acc = jnp.zeros((tm, D_MODEL), F32)
    for j in range(FFN_NCHUNK):
        ub_ref[0:SUBLANES, :] = tail_ref[j]
        ub_ref[SUBLANES:SUBLANES + tm, :] = _dot(h, wu_ref[j])
        tail_ref[j] = ub_ref[tm:tm + SUBLANES, :]
        cv = cb_ref[j]
        for t in range(FFN_CONV):
            off = SUBLANES - (FFN_CONV - 1) + t
            cv = cv + ub_ref[off:off + tm, :] * cw_ref[j, t:t + 1, :]
        act = _silu(cv[:, :FFN_CHUNK]) * cv[:, FFN_CHUNK:]
        acc = acc + _dot(act.astype(BF16), wd_ref[j * FFN_CHUNK:(j + 1) * FFN_CHUNK, :])
    y = x + acc
    if final_norm:
        y = y * lax.rsqrt(jnp.mean(y * y, axis=-1, keepdims=True) + EPS) * fw_ref[...]
    o_ref[...] = y


def _ffn(x2, norm_w, w_up, conv_w, conv_b, w_down, final_w, batch, seq, tm, final_norm):
    nt = seq // tm
    row = lambda b, t: (b * nt + t, 0)
    c2 = lambda b, t: (0, 0)
    c3 = lambda b, t: (0, 0, 0)
    once = pl.Buffered(1)
    return pl.pallas_call(
        functools.partial(_ffn_kernel, tm=tm, final_norm=final_norm),
        grid=(batch, nt),
        in_specs=[pl.BlockSpec((tm, D_MODEL), row),
                  pl.BlockSpec((1, D_MODEL), c2),
                  pl.BlockSpec((FFN_NCHUNK, D_MODEL, 2 * FFN_CHUNK), c3, pipeline_mode=once),
                  pl.BlockSpec((FFN_NCHUNK, SUBLANES, 2 * FFN_CHUNK), c3),
                  pl.BlockSpec((FFN_NCHUNK, 1, 2 * FFN_CHUNK), c3),
                  pl.BlockSpec((D_FF, D_MODEL), c2, pipeline_mode=once),
                  pl.BlockSpec((1, D_MODEL), c2)],
        out_specs=pl.BlockSpec((tm, D_MODEL), row),
        out_shape=jax.ShapeDtypeStruct(x2.shape, F32),
        scratch_shapes=[pltpu.VMEM((tm + SUBLANES, 2 * FFN_CHUNK), F32),
                        pltpu.VMEM((FFN_NCHUNK, SUBLANES, 2 * FFN_CHUNK), F32)],
        compiler_params=_params(("parallel", "arbitrary")),
        name="ffn",
    )(x2, norm_w, w_up, conv_w, conv_b, w_down, final_w)


def _ffn_chunked(t):
    lead = t.shape[:-1]
    t = t.reshape(lead + (2, FFN_NCHUNK, FFN_CHUNK))
    t = jnp.moveaxis(t, -2, 0)
    return t.reshape((FFN_NCHUNK,) + lead + (2 * FFN_CHUNK,))


def kernel(x, attn_norm_w, w_in, gdn_conv_w, gdn_a_log, gdn_dt_bias, gdn_norm_w, s5_lambda_re, s5_lambda_im, s5_log_dt, s5_b_re, s5_b_im, s5_c_re, s5_c_im, s5_d, s5_w_glu, s5_b_glu, s5_norm_w, diff_lambda_q1, diff_lambda_k1, diff_lambda_q2, diff_lambda_k2, diff_norm_w, w_out, ffn_norm_w, ffn_w_up, ffn_conv_w, ffn_conv_b, ffn_w_down, final_norm_w):
    batch, seq, _ = x.shape
    depth = w_in.shape[0]
    tm = min(512, seq)
    tl = min(256, seq)
    tq = min(256, seq)
    ts = min(32, seq)
    assert seq % tm == 0 and seq % tl == 0 and seq % tq == 0 and seq % ts == 0
    assert tl % GDN_CHUNK == 0 and batch % SUBLANES == 0

    x2 = x.reshape(batch * seq, D_MODEL).astype(F32)
    o_qkv, o_z, o_b, o_a, o_su = 0, 3 * GDN_WIDTH, 4 * GDN_WIDTH, 4 * GDN_WIDTH + GDN_HEADS, 4 * GDN_WIDTH + 2 * GDN_HEADS
    o_dq = o_su + S5_WIDTH
    for i in range(depth):
        wi = w_in[i]
        pad = jnp.zeros((D_MODEL, LANES - 2 * GDN_HEADS), wi.dtype)
        wf = jnp.concatenate([wi[:, o_qkv:o_b], wi[:, o_b:o_su], pad], axis=1).astype(BF16)
        ws = wi[:, o_su:o_dq].astype(BF16)
        wa = wi[:, o_dq:].astype(BF16)
        inf, su_tm, dqkv = _inproj(x2, attn_norm_w[i][None, :], wf, ws, wa, batch, seq, tm)

        gate_p = jnp.zeros((SUBLANES, LANES), F32)
        gate_p = gate_p.at[0, GDN_HEADS:2 * GDN_HEADS].set(gdn_a_log[i].astype(F32))
        gate_p = gate_p.at[1, GDN_HEADS:2 * GDN_HEADS].set(gdn_dt_bias[i].astype(F32))
        o_gdn = _gdn(inf, gdn_conv_w[i], gate_p, gdn_norm_w[i][None, :].astype(F32), batch, seq, tl)

        a_bar, bre, bim, cre, cim = _s5_operators(s5_lambda_re[i], s5_lambda_im[i], s5_log_dt[i],
                                                  s5_b_re[i], s5_b_im[i], s5_c_re[i], s5_c_im[i])
        o_s5 = _s5(su_tm.reshape(seq * batch, S5_WIDTH), a_bar, bre, bim, cre, cim,
                   s5_d[i][None, :], s5_w_glu[i].astype(BF16), s5_b_glu[i][None, :],
                   s5_norm_w[i][None, :], batch, seq, ts)

        lam_init = 0.8 - 0.6 * math.exp(-0.3 * i)
        lam_p = jnp.zeros((SUBLANES, LANES), F32)
        for r, t in enumerate((diff_lambda_q1, diff_lambda_k1, diff_lambda_q2, diff_lambda_k2)):
            lam_p = lam_p.at[r, :DIFF_DIM].set(t[i].astype(F32))
        o_diff = _diff_attn(dqkv, lam_p, diff_norm_w[i][None, :], batch, seq, tq, lam_init)

        x2 = _outproj(x2, o_gdn, o_s5.reshape(seq, batch * S5_WIDTH), o_diff, w_out[i].astype(BF16),
                      batch, seq, tm)

        cw = jnp.zeros((FFN_NCHUNK, SUBLANES, 2 * FFN_CHUNK), F32)
        cw = cw.at[:, :FFN_CONV, :].set(_ffn_chunked(ffn_conv_w[i]))
        x2 = _ffn(x2, ffn_norm_w[i][None, :], _ffn_chunked(ffn_w_up[i]).astype(BF16), cw,
                  _ffn_chunked(ffn_conv_b[i][None, :]), ffn_w_down[i].astype(BF16),
                  final_norm_w[None, :], batch, seq, tm, final_norm=(i == depth - 1))
    return x2.reshape(batch, seq, D_MODEL)
```

```python
import functools
import math

import jax
import jax.numpy as jnp
from jax import lax
from jax.experimental import pallas as pl
from jax.experimental.pallas import tpu as pltpu

F32 = jnp.float32
BF16 = jnp.bfloat16
HIGHEST = lax.Precision.HIGHEST

D_MODEL = 1024
EPS = 1e-6
GDN_HEADS = 4
GDN_DIM = 128
GDN_WIDTH = GDN_HEADS * GDN_DIM
GDN_CONV = 4
GDN_CHUNK = 64
GDN_INV_BLOCK = 16
S5_GROUP = 16
S5_GROUPS = 32
S5_WIDTH = S5_GROUPS * S5_GROUP
S5_STATE = 64
S5_NSTATE = S5_GROUPS * S5_STATE
S5_HALF_IN = S5_WIDTH // 2
S5_HALF_ST = S5_NSTATE // 2
DIFF_HEADS = 4
DIFF_DIM = 64
DIFF_V = 2 * DIFF_DIM
DIFF_WIDTH = DIFF_HEADS * DIFF_V
MIX_WIDTH = GDN_WIDTH + S5_WIDTH + DIFF_WIDTH
D_FF = 2816
FFN_CONV = 3
FFN_CHUNK = 256
FFN_NCHUNK = D_FF // FFN_CHUNK

LANES = 128
SUBLANES = 8
VMEM_LIMIT = 56 * 1024 * 1024

INF_QKV = 3 * GDN_WIDTH
INF_WIDTH = INF_QKV + GDN_WIDTH + LANES


def _dot(a, b):
    return jnp.dot(a, b, preferred_element_type=F32)


def _dot_nt(a, b, precision=None):
    return lax.dot_general(a, b, (((1,), (1,)), ((), ())), precision=precision,
                           preferred_element_type=F32)


def _dot_tn(a, b):
    return lax.dot_general(a, b, (((0,), (0,)), ((), ())), preferred_element_type=F32)


def _sigmoid(x):
    return 1.0 / (1.0 + jnp.exp(-x))


def _silu(x):
    return x * _sigmoid(x)


def _params(sem):
    return pltpu.CompilerParams(dimension_semantics=sem, vmem_limit_bytes=VMEM_LIMIT)


def _inproj_kernel(x_ref, nw_ref, wf_ref, ws_ref, wa_ref, of_ref, os_ref, oa_ref):
    x = x_ref[...]
    ms = jnp.mean(x * x, axis=-1, keepdims=True)
    h = ((x * lax.rsqrt(ms + EPS)) * nw_ref[...]).astype(BF16)
    of_ref[...] = _dot(h, wf_ref[...])
    os_ref[...] = _dot(h, ws_ref[...])
    oa_ref[...] = _dot(h, wa_ref[...]).astype(BF16)


def _inproj(x2, nw, wf, ws, wa, batch, seq, tm):
    nt = seq // tm
    tokens = batch * seq
    row = lambda b, t: (b * nt + t, 0)
    const = lambda b, t: (0, 0)
    return pl.pallas_call(
        _inproj_kernel,
        grid=(batch, nt),
        in_specs=[pl.BlockSpec((tm, D_MODEL), row),
                  pl.BlockSpec((1, D_MODEL), const),
                  pl.BlockSpec((D_MODEL, INF_WIDTH), const),
                  pl.BlockSpec((D_MODEL, S5_WIDTH), const),
                  pl.BlockSpec((D_MODEL, 3 * DIFF_WIDTH), const)],
        out_specs=[pl.BlockSpec((tm, INF_WIDTH), row),
                   pl.BlockSpec((tm, S5_WIDTH), lambda b, t: (t, b)),
                   pl.BlockSpec((tm, 3 * DIFF_WIDTH), row)],
        out_shape=[jax.ShapeDtypeStruct((tokens, INF_WIDTH), F32),
                   jax.ShapeDtypeStruct((seq, batch * S5_WIDTH), F32),
                   jax.ShapeDtypeStruct((tokens, 3 * DIFF_WIDTH), BF16)],
        compiler_params=_params(("parallel", "parallel")),
        name="inproj",
    )(x2, nw, wf, ws, wa)


def _unit_lower_inverse(mats, row, col):
    shift = int(math.log2(GDN_INV_BLOCK))
    same = (row >> shift) == (col >> shift)
    eye = (row == col).astype(F32)
    mm = lambda ps, qs: [_dot(p.astype(BF16), q.astype(BF16)) for p, q in zip(ps, qs)]
    add = lambda ps, qs: [p + q for p, q in zip(ps, qs)]
    ad = [jnp.where(same, a, 0.0) for a in mats]
    ao = [jnp.where(same, 0.0, a) for a in mats]
    p = [eye - a for a in ad]
    x = mm(ad, ad)
    p = add(p, mm(p, x))
    x = mm(x, x)
    p = add(p, mm(p, x))
    x = mm(x, x)
    td = add(p, mm(p, x))
    n1 = mm(td, ao)
    n2 = mm(n1, n1)
    n3 = mm(n1, n2)
    return mm([eye - a + b - d for a, b, d in zip(n1, n2, n3)], td)


def _gdn_kernel(qkv_ref, z_ref, ba_ref, cw_ref, gp_ref, nw_ref, o_ref, xs_ref, q_s, k_s, v_s, st_ref,
                *, tl):
    c = GDN_CHUNK
    ti = pl.program_id(1)

    @pl.when(ti == 0)
    def _():
        xs_ref[0:SUBLANES, :] = jnp.zeros((SUBLANES, INF_QKV), F32)
        st_ref[...] = jnp.zeros_like(st_ref)

    xs_ref[SUBLANES:SUBLANES + tl, :] = qkv_ref[...]
    for h in range(GDN_HEADS):
        for part, dst in enumerate((q_s, k_s, v_s)):
            lo = part * GDN_WIDTH + h * GDN_DIM
            acc = jnp.zeros((tl, GDN_DIM), F32)
            for j in range(GDN_CONV):
                off = SUBLANES - (GDN_CONV - 1) + j
                acc = acc + xs_ref[off:off + tl, lo:lo + GDN_DIM] * cw_ref[j:j + 1, lo:lo + GDN_DIM]
            y = _silu(acc)
            if part < 2:
                y = y * lax.rsqrt(jnp.sum(y * y, axis=-1, keepdims=True) + 1e-6)
            if part == 0:
                y = y * (GDN_DIM ** -0.5)
            dst[h] = y
    xs_ref[0:SUBLANES, :] = xs_ref[tl:tl + SUBLANES, :]

    ba = ba_ref[...]
    beta_all = _sigmoid(ba)
    sp = ba + gp_ref[1:2, :]
    softplus = jnp.maximum(sp, 0.0) + jnp.log(1.0 + jnp.exp(-jnp.abs(sp)))
    g_all = -jnp.exp(gp_ref[0:1, :]) * softplus

    row = lax.broadcasted_iota(jnp.int32, (c, c), 0)
    col = lax.broadcasted_iota(jnp.int32, (c, c), 1)
    incl = row >= col
    strict = row > col
    tri = incl.astype(F32)
    sel_r = lax.broadcasted_iota(jnp.int32, (SUBLANES, LANES), 0)
    sel_c = lax.broadcasted_iota(jnp.int32, (SUBLANES, LANES), 1)
    sel = (sel_c == sel_r + GDN_HEADS).astype(F32)

    nc = tl // c
    nh = GDN_HEADS
    units = [(ci, h) for ci in range(nc) for h in range(nh)]

    def rows(ci):
        return slice(ci * c, (ci + 1) * c)

    gcol = [jnp.dot(tri, g_all[rows(ci), :], precision=HIGHEST, preferred_element_type=F32)
            for ci in range(nc)]
    grow = [_dot_nt(sel, gcol[ci], precision=HIGHEST) for ci in range(nc)]

    q = [q_s[h, rows(ci), :] for ci, h in units]
    k = [k_s[h, rows(ci), :] for ci, h in units]
    v = [v_s[h, rows(ci), :] for ci, h in units]
    beta = [beta_all[rows(ci), h:h + 1] for ci, h in units]
    gc = [gcol[ci][:, nh + h:nh + h + 1] for ci, h in units]
    gr = [grow[ci][h:h + 1, :] for ci, h in units]
    decay = [jnp.where(incl, jnp.exp(jnp.where(incl, a - b, 0.0)), 0.0) for a, b in zip(gc, gr)]
    egc = [jnp.exp(a) for a in gc]
    kb = [a * b for a, b in zip(k, beta)]
    kbf = [a.astype(BF16) for a in k]
    kk = [_dot_nt(a.astype(BF16), b) for a, b in zip(kb, kbf)]
    qk = [_dot_nt(a.astype(BF16), b) for a, b in zip(q, kbf)]
    a_low = [jnp.where(strict, a * d, 0.0) for a, d in zip(kk, decay)]
    attn = [jnp.where(incl, a * d, 0.0).astype(BF16) for a, d in zip(qk, decay)]
    t_inv = [t.astype(BF16) for t in _unit_lower_inverse(a_low, row, col)]
    u_mat = [_dot(t, (a * b).astype(BF16)) for t, a, b in zip(t_inv, v, beta)]
    w_mat = [_dot(t, (a * e).astype(BF16)).astype(BF16) for t, a, e in zip(t_inv, kb, egc)]
    qg = [(a * e).astype(BF16) for a, e in zip(q, egc)]
    g_last = [a[c - 1:c, :] for a in gc]
    kd = [(a * jnp.exp(gl - g)).astype(BF16) for a, gl, g in zip(k, g_last, gc)]
    eg_last = [jnp.exp(gl) for gl in g_last]

    state = [st_ref[h] for h in range(nh)]
    for ci in range(nc):
        us = [ci * nh + h for h in range(nh)]
        sb = [s.astype(BF16) for s in state]
        ws = [_dot(w_mat[u], sb[h]) for h, u in enumerate(us)]
        qs = [_dot(qg[u], sb[h]) for h, u in enumerate(us)]
        v_new = [(u_mat[u] - ws[h]).astype(BF16) for h, u in enumerate(us)]
        o = [qs[h] + _dot(attn[u], v_new[h]) for h, u in enumerate(us)]
        state = [state[h] * eg_last[u] + _dot_tn(kd[u], v_new[h]) for h, u in enumerate(us)]
        for h in range(nh):
            on = o[h] * lax.rsqrt(jnp.mean(o[h] * o[h], axis=-1, keepdims=True) + EPS) * nw_ref[...]
            zz = z_ref[rows(ci), h * GDN_DIM:(h + 1) * GDN_DIM]
            o_ref[rows(ci), h * GDN_DIM:(h + 1) * GDN_DIM] = on * _silu(zz)
    for h in range(nh):
        st_ref[h] = state[h]


def _gdn(inf, conv_w, gate_p, norm_w, batch, seq, tl):
    nt = seq // tl
    tokens = batch * seq
    row = lambda b, t: (b * nt + t, 0)
    const = lambda b, t: (0, 0)
    return pl.pallas_call(
        functools.partial(_gdn_kernel, tl=tl),
        grid=(batch, nt),
        in_specs=[pl.BlockSpec((tl, INF_QKV), row),
                  pl.BlockSpec((tl, GDN_WIDTH), lambda b, t: (b * nt + t, INF_QKV // GDN_WIDTH)),
                  pl.BlockSpec((tl, LANES), lambda b, t: (b * nt + t, (INF_QKV + GDN_WIDTH) // LANES)),
                  pl.BlockSpec((GDN_CONV, INF_QKV), const),
                  pl.BlockSpec((SUBLANES, LANES), const),
                  pl.BlockSpec((1, GDN_DIM), const)],
        out_specs=pl.BlockSpec((tl, GDN_WIDTH), row),
        out_shape=jax.ShapeDtypeStruct((tokens, GDN_WIDTH), F32),
        scratch_shapes=[pltpu.VMEM((tl + SUBLANES, INF_QKV), F32),
                        pltpu.VMEM((GDN_HEADS, tl, GDN_DIM), F32),
                        pltpu.VMEM((GDN_HEADS, tl, GDN_DIM), F32),
                        pltpu.VMEM((GDN_HEADS, tl, GDN_DIM), F32),
                        pltpu.VMEM((GDN_HEADS, GDN_DIM, GDN_DIM), F32)],
        compiler_params=_params(("parallel", "arbitrary")),
        name="gdn",
    )(inf, inf, inf, conv_w, gate_p, norm_w)


S5_SCAN_LANES = 512


def _gelu_tanh(x):
    return 0.5 * x * (1.0 + jnp.tanh(math.sqrt(2.0 / math.pi) * (x + 0.044715 * (x * x * x))))


def _s5_kernel(u_ref, a_ref, bre_ref, bim_ref, cre_ref, cim_ref, d_ref, wg_ref, bg_ref, nw_ref, o_ref,
               sre_ref, sim_ref, *, batch, tl):
    rows = batch * tl

    @pl.when(pl.program_id(0) == 0)
    def _():
        sre_ref[0:batch, :] = jnp.zeros((batch, S5_NSTATE), F32)
        sim_ref[0:batch, :] = jnp.zeros((batch, S5_NSTATE), F32)

    u = u_ref[...]
    ub = u.astype(BF16)
    for hf in range(2):
        cin = slice(hf * S5_HALF_IN, (hf + 1) * S5_HALF_IN)
        cst = slice(hf * S5_HALF_ST, (hf + 1) * S5_HALF_ST)
        sre_ref[batch:batch + rows, cst] = _dot(ub[:, cin], bre_ref[hf])
        sim_ref[batch:batch + rows, cst] = _dot(ub[:, cin], bim_ref[hf])

    for lc in range(S5_NSTATE // S5_SCAN_LANES):
        cols = slice(lc * S5_SCAN_LANES, (lc + 1) * S5_SCAN_LANES)
        a_re = jnp.broadcast_to(a_ref[0:1, cols], (batch, S5_SCAN_LANES))
        a_im = jnp.broadcast_to(a_ref[1:2, cols], (batch, S5_SCAN_LANES))

        def step(t, carry):
            s_re, s_im = carry
            r = pl.ds(pl.multiple_of((t + 1) * batch, batch), batch)
            n_re = a_re * s_re - a_im * s_im + sre_ref[r, cols]
            n_im = a_re * s_im + a_im * s_re + sim_ref[r, cols]
            sre_ref[r, cols] = n_re
            sim_ref[r, cols] = n_im
            return n_re, n_im

        s_re, s_im = lax.fori_loop(0, tl, step, (sre_ref[0:batch, cols], sim_ref[0:batch, cols]))
        sre_ref[0:batch, cols] = s_re
        sim_ref[0:batch, cols] = s_im

    ys = []
    for hf in range(2):
        cst = slice(hf * S5_HALF_ST, (hf + 1) * S5_HALF_ST)
        s_re = sre_ref[batch:batch + rows, cst].astype(BF16)
        s_im = sim_ref[batch:batch + rows, cst].astype(BF16)
        ys.append(_dot(s_re, cre_ref[hf]) - _dot(s_im, cim_ref[hf]))
    y = jnp.concatenate(ys, axis=-1) + d_ref[...] * u
    y = _gelu_tanh(y)
    y = y * _sigmoid(_dot(y.astype(BF16), wg_ref[...]) + bg_ref[...])
    o_ref[...] = y * lax.rsqrt(jnp.mean(y * y, axis=-1, keepdims=True) + EPS) * nw_ref[...]


def _s5(u_tm, a_bar, bre, bim, cre, cim, d_skip, w_glu, b_glu, norm_w, batch, seq, tl):
    rows = batch * tl
    blk = lambda t: (t, 0)
    c2 = lambda t: (0, 0)
    c3 = lambda t: (0, 0, 0)
    return pl.pallas_call(
        functools.partial(_s5_kernel, batch=batch, tl=tl),
        grid=(seq // tl,),
        in_specs=[pl.BlockSpec((rows, S5_WIDTH), blk),
                  pl.BlockSpec((SUBLANES, S5_NSTATE), c2),
                  pl.BlockSpec((2, S5_HALF_IN, S5_HALF_ST), c3),
                  pl.BlockSpec((2, S5_HALF_IN, S5_HALF_ST), c3),
                  pl.BlockSpec((2, S5_HALF_ST, S5_HALF_IN), c3),
                  pl.BlockSpec((2, S5_HALF_ST, S5_HALF_IN), c3),
                  pl.BlockSpec((1, S5_WIDTH), c2),
                  pl.BlockSpec((S5_WIDTH, S5_WIDTH), c2),
                  pl.BlockSpec((1, S5_WIDTH), c2),
                  pl.BlockSpec((1, S5_WIDTH), c2)],
        out_specs=pl.BlockSpec((rows, S5_WIDTH), blk),
        out_shape=jax.ShapeDtypeStruct((seq * batch, S5_WIDTH), F32),
        scratch_shapes=[pltpu.VMEM((batch + rows, S5_NSTATE), F32),
                        pltpu.VMEM((batch + rows, S5_NSTATE), F32)],
        compiler_params=_params(("arbitrary",)),
        name="s5",
    )(u_tm, a_bar, bre, bim, cre, cim, d_skip, w_glu, b_glu, norm_w)


def _s5_operators(lam_re, lam_im, log_dt, b_re, b_im, c_re, c_im):
    l_re = jnp.minimum(lam_re.astype(F32), -1e-4)
    l_im = lam_im.astype(F32)
    dt = jnp.exp(log_dt.astype(F32))[:, None]
    mag = jnp.exp(l_re * dt)
    a_re = mag * jnp.cos(l_im * dt)
    a_im = mag * jnp.sin(l_im * dt)
    den = l_re * l_re + l_im * l_im
    f_re = ((a_re - 1.0) * l_re + a_im * l_im) / den
    f_im = (a_im * l_re - (a_re - 1.0) * l_im) / den
    bb_re = f_re[..., None] * b_re.astype(F32) - f_im[..., None] * b_im.astype(F32)
    bb_im = f_re[..., None] * b_im.astype(F32) + f_im[..., None] * b_re.astype(F32)
    a_bar = jnp.zeros((SUBLANES, S5_NSTATE), F32)
    a_bar = a_bar.at[0].set(a_re.reshape(-1)).at[1].set(a_im.reshape(-1))
    eye = jnp.eye(S5_GROUPS, dtype=F32)

    def b_op(t):
        full = jnp.einsum('gph,gk->ghkp', t, eye).reshape(S5_WIDTH, S5_NSTATE)
        return jnp.stack([full[:S5_HALF_IN, :S5_HALF_ST], full[S5_HALF_IN:, S5_HALF_ST:]]).astype(BF16)

    def c_op(t):
        full = jnp.einsum('ghp,gk->gpkh', t, eye).reshape(S5_NSTATE, S5_WIDTH)
        return jnp.stack([full[:S5_HALF_ST, :S5_HALF_IN], full[S5_HALF_ST:, S5_HALF_IN:]]).astype(BF16)

    return a_bar, b_op(bb_re), b_op(bb_im), c_op(c_re.astype(F32)), c_op(c_im.astype(F32))


def _diff_kernel(q_ref, k_ref, v_ref, lp_ref, nw_ref, o_ref, m_ref, l_ref, acc_ref, *, tq, lam_init):
    qi = pl.program_id(1)
    nh = DIFF_HEADS
    lane = lax.broadcasted_iota(jnp.int32, (tq, DIFF_V), 1)
    scale = DIFF_DIM ** -0.5
    qs = []
    for h in range(nh):
        q = q_ref[:, h * DIFF_V:(h + 1) * DIFF_V]
        zero = jnp.zeros_like(q)
        q0 = jnp.where(lane < DIFF_DIM, q, zero)
        q1 = jnp.where(lane < DIFF_DIM, zero, q)
        qs.append((jnp.concatenate([q0, q1], axis=0).astype(F32) * scale).astype(BF16))

    m_ref[...] = jnp.full_like(m_ref, -jnp.inf)
    l_ref[...] = jnp.zeros_like(l_ref)
    acc_ref[...] = jnp.zeros_like(acc_ref)

    def update(kb, masked):
        r = pl.ds(pl.multiple_of(kb * tq, tq), tq)
        hs = range(nh)
        kk = [k_ref[r, h * DIFF_V:(h + 1) * DIFF_V] for h in hs]
        vv = [v_ref[r, h * DIFF_V:(h + 1) * DIFF_V] for h in hs]
        s = [_dot_nt(qs[h], kk[h]) for h in hs]
        if masked:
            rr = lax.broadcasted_iota(jnp.int32, (2 * tq, tq), 0)
            cc = lax.broadcasted_iota(jnp.int32, (2 * tq, tq), 1)
            keep = jnp.where(rr >= tq, rr - tq, rr) >= cc
            s = [jnp.where(keep, a, -jnp.inf) for a in s]
        halves = [[a[:, j * LANES:(j + 1) * LANES] for j in range(tq // LANES)] for a in s]
        m_cur = [functools.reduce(jnp.maximum, hv) for hv in halves]
        m_cur = [jnp.max(a, axis=-1, keepdims=True) for a in m_cur]
        m_prev = [m_ref[h] for h in hs]
        m_new = [jnp.maximum(a, b) for a, b in zip(m_prev, m_cur)]
        alpha = [jnp.exp(a - b) for a, b in zip(m_prev, m_new)]
        p = [[jnp.exp(a - mn) for a in hv] for hv, mn in zip(halves, m_new)]
        for h in hs:
            l_ref[h] = alpha[h] * l_ref[h] + functools.reduce(jnp.add, p[h])
        pb = [jnp.concatenate([a.astype(BF16) for a in ph], axis=-1) for ph in p]
        pv = [_dot(pb[h], vv[h]) for h in hs]
        for h in hs:
            acc_ref[h] = alpha[h] * acc_ref[h] + pv[h]
            m_ref[h] = m_new[h]

    def body(kb, carry):
        update(kb, False)
        return carry

    lax.fori_loop(0, qi, body, 0)
    update(qi, True)

    lp = lp_ref[...]
    lam = (jnp.exp(jnp.sum(lp[0:1, :] * lp[1:2, :], axis=-1, keepdims=True))
           - jnp.exp(jnp.sum(lp[2:3, :] * lp[3:4, :], axis=-1, keepdims=True)) + lam_init)
    for h in range(nh):
        o = acc_ref[h] / jnp.sum(l_ref[h], axis=-1, keepdims=True)
        o = o[:tq, :] - lam * o[tq:, :]
        on = o * lax.rsqrt(jnp.mean(o * o, axis=-1, keepdims=True) + EPS) * nw_ref[...]
        o_ref[:, h * DIFF_V:(h + 1) * DIFF_V] = on * (1.0 - lam_init)


def _diff_attn(qkv, lam_p, norm_w, batch, seq, tq, lam_init):
    nq = seq // tq
    tokens = batch * seq
    return pl.pallas_call(
        functools.partial(_diff_kernel, tq=tq, lam_init=lam_init),
        grid=(batch, nq),
        in_specs=[pl.BlockSpec((tq, DIFF_WIDTH), lambda b, i: (b * nq + i, 0)),
                  pl.BlockSpec((seq, DIFF_WIDTH), lambda b, i: (b, 1)),
                  pl.BlockSpec((seq, DIFF_WIDTH), lambda b, i: (b, 2)),
                  pl.BlockSpec((SUBLANES, LANES), lambda b, i: (0, 0)),
                  pl.BlockSpec((1, DIFF_V), lambda b, i: (0, 0))],
        out_specs=pl.BlockSpec((tq, DIFF_WIDTH), lambda b, i: (b * nq + i, 0)),
        out_shape=jax.ShapeDtypeStruct((tokens, DIFF_WIDTH), F32),
        scratch_shapes=[pltpu.VMEM((DIFF_HEADS, 2 * tq, LANES), F32),
                        pltpu.VMEM((DIFF_HEADS, 2 * tq, LANES), F32),
                        pltpu.VMEM((DIFF_HEADS, 2 * tq, DIFF_V), F32)],
        compiler_params=_params(("parallel", "parallel")),
        name="diff_attn",
    )(qkv, qkv, qkv, lam_p, norm_w)


def _outproj_kernel(x_ref, og_ref, os_ref, od_ref, w_ref, o_ref):
    acc = _dot(og_ref[...].astype(BF16), w_ref[0:GDN_WIDTH, :])
    acc = acc + _dot(os_ref[...].astype(BF16), w_ref[GDN_WIDTH:GDN_WIDTH + S5_WIDTH, :])
    acc = acc + _dot(od_ref[...].astype(BF16), w_ref[GDN_WIDTH + S5_WIDTH:MIX_WIDTH, :])
    o_ref[...] = x_ref[...] + acc


def _outproj(x2, o_gdn, o_s5_tm, o_diff, w_out, batch, seq, tm):
    nt = seq // tm
    row = lambda b, t: (b * nt + t, 0)
    return pl.pallas_call(
        _outproj_kernel,
        grid=(batch, nt),
        in_specs=[pl.BlockSpec((tm, D_MODEL), row),
                  pl.BlockSpec((tm, GDN_WIDTH), row),
                  pl.BlockSpec((tm, S5_WIDTH), lambda b, t: (t, b)),
                  pl.BlockSpec((tm, DIFF_WIDTH), row),
                  pl.BlockSpec((MIX_WIDTH, D_MODEL), lambda b, t: (0, 0))],
        out_specs=pl.BlockSpec((tm, D_MODEL), row),
        out_shape=jax.ShapeDtypeStruct(x2.shape, F32),
        compiler_params=_params(("parallel", "parallel")),
        name="outproj",
    )(x2, o_gdn, o_s5_tm, o_diff, w_out)


def _ffn_kernel(x_ref, nw_ref, wu_ref, cw_ref, cb_ref, wd_ref, fw_ref, o_ref, ub_ref, tail_ref,
                *, tm, final_norm):
    @pl.when(pl.program_id(1) == 0)
    def _():
        tail_ref[...] = jnp.zeros_like(tail_ref)

    x = x_ref[...]
    ms = jnp.mean(x * x, axis=-1, keepdims=True)
    h = ((x * lax.rsqrt(ms + EPS)) * nw_ref[...]).astype(BF16)
    acc = jnp.zeros((tm, D_MODEL), F32)
    for j in range(FFN_NCHUNK):
        ub_ref[0:SUBLANES, :] = tail_ref[j]
        ub_ref[SUBLANES:SUBLANES + tm, :] = _dot(h, wu_ref[j])
        tail_ref[j] = ub_ref[tm:tm + SUBLANES, :]
        cv = cb_ref[j]
        for t in range(FFN_CONV):
            off = SUBLANES - (FFN_CONV - 1) + t
            cv = cv + ub_ref[off:off + tm, :] * cw_ref[j, t:t + 1, :]
        act = _silu(cv[:, :FFN_CHUNK]) * cv[:, FFN_CHUNK:]
        acc = acc + _dot(act.astype(BF16), wd_ref[j * FFN_CHUNK:(j + 1) * FFN_CHUNK, :])
    y = x + acc
    if final_norm:
        y = y * lax.rsqrt(jnp.mean(y * y, axis=-1, keepdims=True) + EPS) * fw_ref[...]
    o_ref[...] = y


def _ffn(x2, norm_w, w_up, conv_w, conv_b, w_down, final_w, batch, seq, tm, final_norm):
    nt = seq // tm
    row = lambda b, t: (b * nt + t, 0)
    c2 = lambda b, t: (0, 0)
    c3 = lambda b, t: (0, 0, 0)
    once = pl.Buffered(1)
    return pl.pallas_call(
        functools.partial(_ffn_kernel, tm=tm, final_norm=final_norm),
        grid=(batch, nt),
        in_specs=[pl.BlockSpec((tm, D_MODEL), row),
                  pl.BlockSpec((1, D_MODEL), c2),
                  pl.BlockSpec((FFN_NCHUNK, D_MODEL, 2 * FFN_CHUNK), c3, pipeline_mode=once),
                  pl.BlockSpec((FFN_NCHUNK, SUBLANES, 2 * FFN_CHUNK), c3),
                  pl.BlockSpec((FFN_NCHUNK, 1, 2 * FFN_CHUNK), c3),
                  pl.BlockSpec((D_FF, D_MODEL), c2, pipeline_mode=once),
                  pl.BlockSpec((1, D_MODEL), c2)],
        out_specs=pl.BlockSpec((tm, D_MODEL), row),
        out_shape=jax.ShapeDtypeStruct(x2.shape, F32),
        scratch_shapes=[pltpu.VMEM((tm + SUBLANES, 2 * FFN_CHUNK), F32),
                        pltpu.VMEM((FFN_NCHUNK, SUBLANES, 2 * FFN_CHUNK), F32)],
        compiler_params=_params(("parallel", "arbitrary")),
        name="ffn",
    )(x2, norm_w, w_up, conv_w, conv_b, w_down, final_w)


def _ffn_chunked(t):
    lead = t.shape[:-1]
    t = t.reshape(lead + (2, FFN_NCHUNK, FFN_CHUNK))
    t = jnp.moveaxis(t, -2, 0)
    return t.reshape((FFN_NCHUNK,) + lead + (2 * FFN_CHUNK,))


def kernel(x, attn_norm_w, w_in, gdn_conv_w, gdn_a_log, gdn_dt_bias, gdn_norm_w, s5_lambda_re, s5_lambda_im, s5_log_dt, s5_b_re, s5_b_im, s5_c_re, s5_c_im, s5_d, s5_w_glu, s5_b_glu, s5_norm_w, diff_lambda_q1, diff_lambda_k1, diff_lambda_q2, diff_lambda_k2, diff_norm_w, w_out, ffn_norm_w, ffn_w_up, ffn_conv_w, ffn_conv_b, ffn_w_down, final_norm_w):
    batch, seq, _ = x.shape
    depth = w_in.shape[0]
    tm = min(512, seq)
    tl = min(256, seq)
    tq = min(256, seq)
    ts = min(32, seq)
    assert seq % tm == 0 and seq % tl == 0 and seq % tq == 0 and seq % ts == 0
    assert tl % GDN_CHUNK == 0 and batch % SUBLANES == 0

    x2 = x.reshape(batch * seq, D_MODEL).astype(F32)
    o_qkv, o_z, o_b, o_a, o_su = 0, 3 * GDN_WIDTH, 4 * GDN_WIDTH, 4 * GDN_WIDTH + GDN_HEADS, 4 * GDN_WIDTH + 2 * GDN_HEADS
    o_dq = o_su + S5_WIDTH
    for i in range(depth):
        wi = w_in[i]
        pad = jnp.zeros((D_MODEL, LANES - 2 * GDN_HEADS), wi.dtype)
        wf = jnp.concatenate([wi[:, o_qkv:o_b], wi[:, o_b:o_su], pad], axis=1).astype(BF16)
        ws = wi[:, o_su:o_dq].astype(BF16)
        wa = wi[:, o_dq:].astype(BF16)
        inf, su_tm, dqkv = _inproj(x2, attn_norm_w[i][None, :], wf, ws, wa, batch, seq, tm)

        gate_p = jnp.zeros((SUBLANES, LANES), F32)
        gate_p = gate_p.at[0, GDN_HEADS:2 * GDN_HEADS].set(gdn_a_log[i].astype(F32))
        gate_p = gate_p.at[1, GDN_HEADS:2 * GDN_HEADS].set(gdn_dt_bias[i].astype(F32))
        o_gdn = _gdn(inf, gdn_conv_w[i], gate_p, gdn_norm_w[i][None, :].astype(F32), batch, seq, tl)

        a_bar, bre, bim, cre, cim = _s5_operators(s5_lambda_re[i], s5_lambda_im[i], s5_log_dt[i],
                                                  s5_b_re[i], s5_b_im[i], s5_c_re[i], s5_c_im[i])
        o_s5 = _s5(su_tm.reshape(seq * batch, S5_WIDTH), a_bar, bre, bim, cre, cim,
                   s5_d[i][None, :], s5_w_glu[i].astype(BF16), s5_b_glu[i][None, :],
                   s5_norm_w[i][None, :], batch, seq, ts)

        lam_init = 0.8 - 0.6 * math.exp(-0.3 * i)
        lam_p = jnp.zeros((SUBLANES, LANES), F32)
        for r, t in enumerate((diff_lambda_q1, diff_lambda_k1, diff_lambda_q2, diff_lambda_k2)):
            lam_p = lam_p.at[r, :DIFF_DIM].set(t[i].astype(F32))
        o_diff = _diff_attn(dqkv, lam_p, diff_norm_w[i][None, :], batch, seq, tq, lam_init)

        x2 = _outproj(x2, o_gdn, o_s5.reshape(seq, batch * S5_WIDTH), o_diff, w_out[i].astype(BF16),
                      batch, seq, tm)

        cw = jnp.zeros((FFN_NCHUNK, SUBLANES, 2 * FFN_CHUNK), F32)
        cw = cw.at[:, :FFN_CONV, :].set(_ffn_chunked(ffn_conv_w[i]))
        x2 = _ffn(x2, ffn_norm_w[i][None, :], _ffn_chunked(ffn_w_up[i]).astype(BF16), cw,
                  _ffn_chunked(ffn_conv_b[i][None, :]), ffn_w_down[i].astype(BF16),
                  final_norm_w[None, :], batch, seq, tm, final_norm=(i == depth - 1))
    return x2.reshape(batch, seq, D_MODEL)
```

```python
import functools
import math

import jax
import jax.numpy as jnp
from jax import lax
from jax.experimental import pallas as pl
from jax.experimental.pallas import tpu as pltpu

F32 = jnp.float32
BF16 = jnp.bfloat16
HIGHEST = lax.Precision.HIGHEST

D_MODEL = 1024
EPS = 1e-6
GDN_HEADS = 4
GDN_DIM = 128
GDN_WIDTH = GDN_HEADS * GDN_DIM
GDN_CONV = 4
GDN_CHUNK = 64
GDN_INV_BLOCK = 16
S5_GROUP = 16
S5_GROUPS = 32
S5_WIDTH = S5_GROUPS * S5_GROUP
S5_STATE = 64
S5_NSTATE = S5_GROUPS * S5_STATE
S5_HALF_IN = S5_WIDTH // 2
S5_HALF_ST = S5_NSTATE // 2
DIFF_HEADS = 4
DIFF_DIM = 64
DIFF_V = 2 * DIFF_DIM
DIFF_WIDTH = DIFF_HEADS * DIFF_V
MIX_WIDTH = GDN_WIDTH + S5_WIDTH + DIFF_WIDTH
D_FF = 2816
FFN_CONV = 3
FFN_CHUNK = 256
FFN_NCHUNK = D_FF // FFN_CHUNK

LANES = 128
SUBLANES = 8
VMEM_LIMIT = 56 * 1024 * 1024

INF_QKV = 3 * GDN_WIDTH
INF_WIDTH = INF_QKV + GDN_WIDTH + LANES


def _dot(a, b):
    return jnp.dot(a, b, preferred_element_type=F32)


def _dot_nt(a, b, precision=None):
    return lax.dot_general(a, b, (((1,), (1,)), ((), ())), precision=precision,
                           preferred_element_type=F32)


def _dot_tn(a, b):
    return lax.dot_general(a, b, (((0,), (0,)), ((), ())), preferred_element_type=F32)


def _sigmoid(x):
    return 1.0 / (1.0 + jnp.exp(-x))


def _silu(x):
    return x * _sigmoid(x)


def _params(sem):
    return pltpu.CompilerParams(dimension_semantics=sem, vmem_limit_bytes=VMEM_LIMIT)


def _inproj_kernel(x_ref, nw_ref, wf_ref, ws_ref, wa_ref, of_ref, os_ref, oa_ref):
    x = x_ref[...]
    ms = jnp.mean(x * x, axis=-1, keepdims=True)
    h = ((x * lax.rsqrt(ms + EPS)) * nw_ref[...]).astype(BF16)
    of_ref[...] = _dot(h, wf_ref[...])
    os_ref[...] = _dot(h, ws_ref[...])
    oa_ref[...] = _dot(h, wa_ref[...]).astype(BF16)


def _inproj(x2, nw, wf, ws, wa, batch, seq, tm):
    nt = seq // tm
    tokens = batch * seq
    row = lambda b, t: (b * nt + t, 0)
    const = lambda b, t: (0, 0)
    return pl.pallas_call(
        _inproj_kernel,
        grid=(batch, nt),
        in_specs=[pl.BlockSpec((tm, D_MODEL), row),
                  pl.BlockSpec((1, D_MODEL), const),
                  pl.BlockSpec((D_MODEL, INF_WIDTH), const),
                  pl.BlockSpec((D_MODEL, S5_WIDTH), const),
                  pl.BlockSpec((D_MODEL, 3 * DIFF_WIDTH), const)],
        out_specs=[pl.BlockSpec((tm, INF_WIDTH), row),
                   pl.BlockSpec((tm, S5_WIDTH), row),
                   pl.BlockSpec((tm, 3 * DIFF_WIDTH), row)],
        out_shape=[jax.ShapeDtypeStruct((tokens, INF_WIDTH), F32),
                   jax.ShapeDtypeStruct((tokens, S5_WIDTH), F32),
                   jax.ShapeDtypeStruct((tokens, 3 * DIFF_WIDTH), BF16)],
        compiler_params=_params(("parallel", "parallel")),
        name="inproj",
    )(x2, nw, wf, ws, wa)


def _unit_lower_inverse(mats, row, col):
    shift = int(math.log2(GDN_INV_BLOCK))
    same = (row >> shift) == (col >> shift)
    eye = (row == col).astype(F32)
    mm = lambda ps, qs: [_dot(p.astype(BF16), q.astype(BF16)) for p, q in zip(ps, qs)]
    add = lambda ps, qs: [p + q for p, q in zip(ps, qs)]
    ad = [jnp.where(same, a, 0.0) for a in mats]
    ao = [jnp.where(same, 0.0, a) for a in mats]
    p = [eye - a for a in ad]
    x = mm(ad, ad)
    p = add(p, mm(p, x))
    x = mm(x, x)
    p = add(p, mm(p, x))
    x = mm(x, x)
    td = add(p, mm(p, x))
    n1 = mm(td, ao)
    n2 = mm(n1, n1)
    n3 = mm(n1, n2)
    return mm([eye - a + b - d for a, b, d in zip(n1, n2, n3)], td)


def _gdn_kernel(qkv_ref, z_ref, ba_ref, cw_ref, gp_ref, nw_ref, o_ref, xs_ref, q_s, k_s, v_s, st_ref,
                *, tl):
    c = GDN_CHUNK
    ti = pl.program_id(1)

    @pl.when(ti == 0)
    def _():
        xs_ref[0:SUBLANES, :] = jnp.zeros((SUBLANES, INF_QKV), F32)
        st_ref[...] = jnp.zeros_like(st_ref)

    xs_ref[SUBLANES:SUBLANES + tl, :] = qkv_ref[...]
    for h in range(GDN_HEADS):
        for part, dst in enumerate((q_s, k_s, v_s)):
            lo = part * GDN_WIDTH + h * GDN_DIM
            acc = jnp.zeros((tl, GDN_DIM), F32)
            for j in range(GDN_CONV):
                off = SUBLANES - (GDN_CONV - 1) + j
                acc = acc + xs_ref[off:off + tl, lo:lo + GDN_DIM] * cw_ref[j:j + 1, lo:lo + GDN_DIM]
            y = _silu(acc)
            if part < 2:
                y = y * lax.rsqrt(jnp.sum(y * y, axis=-1, keepdims=True) + 1e-6)
            if part == 0:
                y = y * (GDN_DIM ** -0.5)
            dst[h] = y
    xs_ref[0:SUBLANES, :] = xs_ref[tl:tl + SUBLANES, :]

    ba = ba_ref[...]
    beta_all = _sigmoid(ba)
    sp = ba + gp_ref[1:2, :]
    softplus = jnp.maximum(sp, 0.0) + jnp.log(1.0 + jnp.exp(-jnp.abs(sp)))
    g_all = -jnp.exp(gp_ref[0:1, :]) * softplus

    row = lax.broadcasted_iota(jnp.int32, (c, c), 0)
    col = lax.broadcasted_iota(jnp.int32, (c, c), 1)
    incl = row >= col
    strict = row > col
    tri = incl.astype(F32)
    sel_r = lax.broadcasted_iota(jnp.int32, (SUBLANES, LANES), 0)
    sel_c = lax.broadcasted_iota(jnp.int32, (SUBLANES, LANES), 1)
    sel = (sel_c == sel_r + GDN_HEADS).astype(F32)

    nc = tl // c
    nh = GDN_HEADS
    units = [(ci, h) for ci in range(nc) for h in range(nh)]

    def rows(ci):
        return slice(ci * c, (ci + 1) * c)

    gcol = [jnp.dot(tri, g_all[rows(ci), :], precision=HIGHEST, preferred_element_type=F32)
            for ci in range(nc)]
    grow = [_dot_nt(sel, gcol[ci], precision=HIGHEST) for ci in range(nc)]

    q = [q_s[h, rows(ci), :] for ci, h in units]
    k = [k_s[h, rows(ci), :] for ci, h in units]
    v = [v_s[h, rows(ci), :] for ci, h in units]
    beta = [beta_all[rows(ci), h:h + 1] for ci, h in units]
    gc = [gcol[ci][:, nh + h:nh + h + 1] for ci, h in units]
    gr = [grow[ci][h:h + 1, :] for ci, h in units]
    decay = [jnp.where(incl, jnp.exp(jnp.where(incl, a - b, 0.0)), 0.0) for a, b in zip(gc, gr)]
    egc = [jnp.exp(a) for a in gc]
    kb = [a * b for a, b in zip(k, beta)]
    kbf = [a.astype(BF16) for a in k]
    kk = [_dot_nt(a.astype(BF16), b) for a, b in zip(kb, kbf)]
    qk = [_dot_nt(a.astype(BF16), b) for a, b in zip(q, kbf)]
    a_low = [jnp.where(strict, a * d, 0.0) for a, d in zip(kk, decay)]
    attn = [jnp.where(incl, a * d, 0.0).astype(BF16) for a, d in zip(qk, decay)]
    t_inv = [t.astype(BF16) for t in _unit_lower_inverse(a_low, row, col)]
    u_mat = [_dot(t, (a * b).astype(BF16)) for t, a, b in zip(t_inv, v, beta)]
    w_mat = [_dot(t, (a * e).astype(BF16)).astype(BF16) for t, a, e in zip(t_inv, kb, egc)]
    qg = [(a * e).astype(BF16) for a, e in zip(q, egc)]
    g_last = [a[c - 1:c, :] for a in gc]
    kd = [(a * jnp.exp(gl - g)).astype(BF16) for a, gl, g in zip(k, g_last, gc)]
    eg_last = [jnp.exp(gl) for gl in g_last]

    state = [st_ref[h] for h in range(nh)]
    for ci in range(nc):
        us = [ci * nh + h for h in range(nh)]
        sb = [s.astype(BF16) for s in state]
        ws = [_dot(w_mat[u], sb[h]) for h, u in enumerate(us)]
        qs = [_dot(qg[u], sb[h]) for h, u in enumerate(us)]
        v_new = [(u_mat[u] - ws[h]).astype(BF16) for h, u in enumerate(us)]
        o = [qs[h] + _dot(attn[u], v_new[h]) for h, u in enumerate(us)]
        state = [state[h] * eg_last[u] + _dot_tn(kd[u], v_new[h]) for h, u in enumerate(us)]
        for h in range(nh):
            on = o[h] * lax.rsqrt(jnp.mean(o[h] * o[h], axis=-1, keepdims=True) + EPS) * nw_ref[...]
            zz = z_ref[rows(ci), h * GDN_DIM:(h + 1) * GDN_DIM]
            o_ref[rows(ci), h * GDN_DIM:(h + 1) * GDN_DIM] = (on * _silu(zz)).astype(BF16)
    for h in range(nh):
        st_ref[h] = state[h]


def _gdn(inf, conv_w, gate_p, norm_w, batch, seq, tl):
    nt = seq // tl
    tokens = batch * seq
    row = lambda b, t: (b * nt + t, 0)
    const = lambda b, t: (0, 0)
    return pl.pallas_call(
        functools.partial(_gdn_kernel, tl=tl),
        grid=(batch, nt),
        in_specs=[pl.BlockSpec((tl, INF_QKV), row),
                  pl.BlockSpec((tl, GDN_WIDTH), lambda b, t: (b * nt + t, INF_QKV // GDN_WIDTH)),
                  pl.BlockSpec((tl, LANES), lambda b, t: (b * nt + t, (INF_QKV + GDN_WIDTH) // LANES)),
                  pl.BlockSpec((GDN_CONV, INF_QKV), const),
                  pl.BlockSpec((SUBLANES, LANES), const),
                  pl.BlockSpec((1, GDN_DIM), const)],
        out_specs=pl.BlockSpec((tl, GDN_WIDTH), row),
        out_shape=jax.ShapeDtypeStruct((tokens, GDN_WIDTH), BF16),
        scratch_shapes=[pltpu.VMEM((tl + SUBLANES, INF_QKV), F32),
                        pltpu.VMEM((GDN_HEADS, tl, GDN_DIM), F32),
                        pltpu.VMEM((GDN_HEADS, tl, GDN_DIM), F32),
                        pltpu.VMEM((GDN_HEADS, tl, GDN_DIM), F32),
                        pltpu.VMEM((GDN_HEADS, GDN_DIM, GDN_DIM), F32)],
        compiler_params=_params(("parallel", "arbitrary")),
        name="gdn",
    )(inf, inf, inf, conv_w, gate_p, norm_w)


S5_SCAN_LANES = 512


def _gelu_tanh(x):
    return 0.5 * x * (1.0 + jnp.tanh(math.sqrt(2.0 / math.pi) * (x + 0.044715 * (x * x * x))))


def _s5_kernel(u_ref, a_ref, bre_ref, bim_ref, cre_ref, cim_ref, d_ref, wg_ref, bg_ref, nw_ref, o_ref,
               sre_ref, sim_ref, *, batch, tl):
    rows = batch * tl

    @pl.when(pl.program_id(0) == 0)
    def _():
        sre_ref[0:batch, :] = jnp.zeros((batch, S5_NSTATE), F32)
        sim_ref[0:batch, :] = jnp.zeros((batch, S5_NSTATE), F32)

    u_bm = u_ref[...].reshape(rows, S5_WIDTH)
    r_i = lax.broadcasted_iota(jnp.int32, (rows, rows), 0)
    c_i = lax.broadcasted_iota(jnp.int32, (rows, rows), 1)
    to_tm = (c_i == (r_i % batch) * tl + r_i // batch).astype(BF16)
    to_bm = (c_i == (r_i % tl) * batch + r_i // tl).astype(BF16)
    u_hi = u_bm.astype(BF16)
    u_lo = (u_bm - u_hi.astype(F32)).astype(BF16)
    u_hi_tm = _dot(to_tm, u_hi)
    u = u_hi_tm + _dot(to_tm, u_lo)
    ub = u_hi_tm.astype(BF16)
    for hf in range(2):
        cin = slice(hf * S5_HALF_IN, (hf + 1) * S5_HALF_IN)
        cst = slice(hf * S5_HALF_ST, (hf + 1) * S5_HALF_ST)
        sre_ref[batch:batch + rows, cst] = _dot(ub[:, cin], bre_ref[hf])
        sim_ref[batch:batch + rows, cst] = _dot(ub[:, cin], bim_ref[hf])

    for lc in range(S5_NSTATE // S5_SCAN_LANES):
        cols = slice(lc * S5_SCAN_LANES, (lc + 1) * S5_SCAN_LANES)
        a_re = jnp.broadcast_to(a_ref[0:1, cols], (batch, S5_SCAN_LANES))
        a_im = jnp.broadcast_to(a_ref[1:2, cols], (batch, S5_SCAN_LANES))

        def step(t, carry):
            s_re, s_im = carry
            r = pl.ds(pl.multiple_of((t + 1) * batch, batch), batch)
            n_re = a_re * s_re - a_im * s_im + sre_ref[r, cols]
            n_im = a_re * s_im + a_im * s_re + sim_ref[r, cols]
            sre_ref[r, cols] = n_re
            sim_ref[r, cols] = n_im
            return n_re, n_im

        s_re, s_im = lax.fori_loop(0, tl, step, (sre_ref[0:batch, cols], sim_ref[0:batch, cols]))
        sre_ref[0:batch, cols] = s_re
        sim_ref[0:batch, cols] = s_im

    ys = []
    for hf in range(2):
        cst = slice(hf * S5_HALF_ST, (hf + 1) * S5_HALF_ST)
        s_re = sre_ref[batch:batch + rows, cst].astype(BF16)
        s_im = sim_ref[batch:batch + rows, cst].astype(BF16)
        ys.append(_dot(s_re, cre_ref[hf]) - _dot(s_im, cim_ref[hf]))
    y = jnp.concatenate(ys, axis=-1) + d_ref[...] * u
    y = _gelu_tanh(y)
    y = y * _sigmoid(_dot(y.astype(BF16), wg_ref[...]) + bg_ref[...])
    y = (y * lax.rsqrt(jnp.mean(y * y, axis=-1, keepdims=True) + EPS) * nw_ref[...]).astype(BF16)
    o_ref[...] = _dot(to_bm, y).astype(BF16).reshape(batch, tl, S5_WIDTH)


def _s5(u3, a_bar, bre, bim, cre, cim, d_skip, w_glu, b_glu, norm_w, batch, seq, tl):
    rows = batch * tl
    blk = lambda t: (0, t, 0)
    c2 = lambda t: (0, 0)
    c3 = lambda t: (0, 0, 0)
    return pl.pallas_call(
        functools.partial(_s5_kernel, batch=batch, tl=tl),
        grid=(seq // tl,),
        in_specs=[pl.BlockSpec((batch, tl, S5_WIDTH), blk),
                  pl.BlockSpec((SUBLANES, S5_NSTATE), c2),
                  pl.BlockSpec((2, S5_HALF_IN, S5_HALF_ST), c3),
                  pl.BlockSpec((2, S5_HALF_IN, S5_HALF_ST), c3),
                  pl.BlockSpec((2, S5_HALF_ST, S5_HALF_IN), c3),
                  pl.BlockSpec((2, S5_HALF_ST, S5_HALF_IN), c3),
                  pl.BlockSpec((1, S5_WIDTH), c2),
                  pl.BlockSpec((S5_WIDTH, S5_WIDTH), c2),
                  pl.BlockSpec((1, S5_WIDTH), c2),
                  pl.BlockSpec((1, S5_WIDTH), c2)],
        out_specs=pl.BlockSpec((batch, tl, S5_WIDTH), blk),
        out_shape=jax.ShapeDtypeStruct((batch, seq, S5_WIDTH), BF16),
        scratch_shapes=[pltpu.VMEM((batch + rows, S5_NSTATE), F32),
                        pltpu.VMEM((batch + rows, S5_NSTATE), F32)],
        compiler_params=_params(("arbitrary",)),
        name="s5",
    )(u3, a_bar, bre, bim, cre, cim, d_skip, w_glu, b_glu, norm_w)


def _s5_operators(lam_re, lam_im, log_dt, b_re, b_im, c_re, c_im):
    l_re = jnp.minimum(lam_re.astype(F32), -1e-4)
    l_im = lam_im.astype(F32)
    dt = jnp.exp(log_dt.astype(F32))[:, None]
    mag = jnp.exp(l_re * dt)
    a_re = mag * jnp.cos(l_im * dt)
    a_im = mag * jnp.sin(l_im * dt)
    den = l_re * l_re + l_im * l_im
    f_re = ((a_re - 1.0) * l_re + a_im * l_im) / den
    f_im = (a_im * l_re - (a_re - 1.0) * l_im) / den
    bb_re = f_re[..., None] * b_re.astype(F32) - f_im[..., None] * b_im.astype(F32)
    bb_im = f_re[..., None] * b_im.astype(F32) + f_im[..., None] * b_re.astype(F32)
    a_bar = jnp.zeros((SUBLANES, S5_NSTATE), F32)
    a_bar = a_bar.at[0].set(a_re.reshape(-1)).at[1].set(a_im.reshape(-1))
    eye = jnp.eye(S5_GROUPS, dtype=F32)

    def b_op(t):
        full = jnp.einsum('gph,gk->ghkp', t, eye).reshape(S5_WIDTH, S5_NSTATE)
        return jnp.stack([full[:S5_HALF_IN, :S5_HALF_ST], full[S5_HALF_IN:, S5_HALF_ST:]]).astype(BF16)

    def c_op(t):
        full = jnp.einsum('ghp,gk->gpkh', t, eye).reshape(S5_NSTATE, S5_WIDTH)
        return jnp.stack([full[:S5_HALF_ST, :S5_HALF_IN], full[S5_HALF_ST:, S5_HALF_IN:]]).astype(BF16)

    return a_bar, b_op(bb_re), b_op(bb_im), c_op(c_re.astype(F32)), c_op(c_im.astype(F32))


def _diff_kernel(q_ref, k_ref, v_ref, lp_ref, nw_ref, o_ref, m_ref, l_ref, acc_ref, *, tq, lam_init):
    qi = pl.program_id(1)
    nh = DIFF_HEADS
    lane = lax.broadcasted_iota(jnp.int32, (tq, DIFF_V), 1)
    scale = DIFF_DIM ** -0.5
    qs = []
    for h in range(nh):
        q = q_ref[:, h * DIFF_V:(h + 1) * DIFF_V]
        zero = jnp.zeros_like(q)
        q0 = jnp.where(lane < DIFF_DIM, q, zero)
        q1 = jnp.where(lane < DIFF_DIM, zero, q)
        qs.append((jnp.concatenate([q0, q1], axis=0).astype(F32) * scale).astype(BF16))

    m_ref[...] = jnp.full_like(m_ref, -jnp.inf)
    l_ref[...] = jnp.zeros_like(l_ref)
    acc_ref[...] = jnp.zeros_like(acc_ref)

    def update(kb, masked):
        r = pl.ds(pl.multiple_of(kb * tq, tq), tq)
        hs = range(nh)
        kk = [k_ref[r, h * DIFF_V:(h + 1) * DIFF_V] for h in hs]
        vv = [v_ref[r, h * DIFF_V:(h + 1) * DIFF_V] for h in hs]
        s = [_dot_nt(qs[h], kk[h]) for h in hs]
        if masked:
            rr = lax.broadcasted_iota(jnp.int32, (2 * tq, tq), 0)
            cc = lax.broadcasted_iota(jnp.int32, (2 * tq, tq), 1)
            keep = jnp.where(rr >= tq, rr - tq, rr) >= cc
            s = [jnp.where(keep, a, -jnp.inf) for a in s]
        halves = [[a[:, j * LANES:(j + 1) * LANES] for j in range(tq // LANES)] for a in s]
        m_cur = [functools.reduce(jnp.maximum, hv) for hv in halves]
        m_cur = [jnp.max(a, axis=-1, keepdims=True) for a in m_cur]
        m_prev = [m_ref[h] for h in hs]
        m_new = [jnp.maximum(a, b) for a, b in zip(m_prev, m_cur)]
        alpha = [jnp.exp(a - b) for a, b in zip(m_prev, m_new)]
        p = [[jnp.exp(a - mn) for a in hv] for hv, mn in zip(halves, m_new)]
        for h in hs:
            l_ref[h] = alpha[h] * l_ref[h] + functools.reduce(jnp.add, p[h])
        pb = [jnp.concatenate([a.astype(BF16) for a in ph], axis=-1) for ph in p]
        pv = [_dot(pb[h], vv[h]) for h in hs]
        for h in hs:
            acc_ref[h] = alpha[h] * acc_ref[h] + pv[h]
            m_ref[h] = m_new[h]

    def body(kb, carry):
        update(kb, False)
        return carry

    lax.fori_loop(0, qi, body, 0)
    update(qi, True)

    lp = lp_ref[...]
    lam = (jnp.exp(jnp.sum(lp[0:1, :] * lp[1:2, :], axis=-1, keepdims=True))
           - jnp.exp(jnp.sum(lp[2:3, :] * lp[3:4, :], axis=-1, keepdims=True)) + lam_init)
    for h in range(nh):
        o = acc_ref[h] / jnp.sum(l_ref[h], axis=-1, keepdims=True)
        o = o[:tq, :] - lam * o[tq:, :]
        on = o * lax.rsqrt(jnp.mean(o * o, axis=-1, keepdims=True) + EPS) * nw_ref[...]
        o_ref[:, h * DIFF_V:(h + 1) * DIFF_V] = (on * (1.0 - lam_init)).astype(BF16)


def _diff_attn(qkv, lam_p, norm_w, batch, seq, tq, lam_init):
    nq = seq // tq
    tokens = batch * seq
    return pl.pallas_call(
        functools.partial(_diff_kernel, tq=tq, lam_init=lam_init),
        grid=(batch, nq),
        in_specs=[pl.BlockSpec((tq, DIFF_WIDTH), lambda b, i: (b * nq + i, 0)),
                  pl.BlockSpec((seq, DIFF_WIDTH), lambda b, i: (b, 1)),
                  pl.BlockSpec((seq, DIFF_WIDTH), lambda b, i: (b, 2)),
                  pl.BlockSpec((SUBLANES, LANES), lambda b, i: (0, 0)),
                  pl.BlockSpec((1, DIFF_V), lambda b, i: (0, 0))],
        out_specs=pl.BlockSpec((tq, DIFF_WIDTH), lambda b, i: (b * nq + i, 0)),
        out_shape=jax.ShapeDtypeStruct((tokens, DIFF_WIDTH), BF16),
        scratch_shapes=[pltpu.VMEM((DIFF_HEADS, 2 * tq, LANES), F32),
                        pltpu.VMEM((DIFF_HEADS, 2 * tq, LANES), F32),
                        pltpu.VMEM((DIFF_HEADS, 2 * tq, DIFF_V), F32)],
        compiler_params=_params(("parallel", "parallel")),
        name="diff_attn",
    )(qkv, qkv, qkv, lam_p, norm_w)


def _outproj_kernel(x_ref, og_ref, os_ref, od_ref, w_ref, o_ref):
    acc = _dot(og_ref[...], w_ref[0:GDN_WIDTH, :])
    acc = acc + _dot(os_ref[...], w_ref[GDN_WIDTH:GDN_WIDTH + S5_WIDTH, :])
    acc = acc + _dot(od_ref[...], w_ref[GDN_WIDTH + S5_WIDTH:MIX_WIDTH, :])
    o_ref[...] = x_ref[...] + acc


def _outproj(x2, o_gdn, o_s5, o_diff, w_out, batch, seq, tm):
    nt = seq // tm
    row = lambda b, t: (b * nt + t, 0)
    return pl.pallas_call(
        _outproj_kernel,
        grid=(batch, nt),
        in_specs=[pl.BlockSpec((tm, D_MODEL), row),
                  pl.BlockSpec((tm, GDN_WIDTH), row),
                  pl.BlockSpec((tm, S5_WIDTH), row),
                  pl.BlockSpec((tm, DIFF_WIDTH), row),
                  pl.BlockSpec((MIX_WIDTH, D_MODEL), lambda b, t: (0, 0))],
        out_specs=pl.BlockSpec((tm, D_MODEL), row),
        out_shape=jax.ShapeDtypeStruct(x2.shape, F32),
        compiler_params=_params(("parallel", "parallel")),
        name="outproj",
    )(x2, o_gdn, o_s5, o_diff, w_out)


def _ffn_kernel(x_ref, nw_ref, wu_ref, cw_ref, cb_ref, wd_ref, fw_ref, o_ref, ub_ref, tail_ref,
                *, tm, final_norm):
    @pl.when(pl.program_id(1) == 0)
    def _():
        tail_ref[...] = jnp.zeros_like(tail_ref)

    x = x_ref[...]
    ms = jnp.mean(x * x, axis=-1, keepdims=True)
    h = ((x * lax.rsqrt(ms + EPS)) * nw_ref[...]).astype(BF16)
    acc = jnp.zeros((tm, D_MODEL), F32)
    ub_ref[0, SUBLANES:SUBLANES + tm, :] = _dot(h, wu_ref[0])
    for j in range(FFN_NCHUNK):
        slot = j % 2
        if j + 1 < FFN_NCHUNK:
            ub_ref[1 - slot, SUBLANES:SUBLANES + tm, :] = _dot(h, wu_ref[j + 1])
        ub_ref[slot, 0:SUBLANES, :] = tail_ref[j]
        tail_ref[j] = ub_ref[slot, tm:tm + SUBLANES, :]
        cv = cb_ref[j]
        for t in range(FFN_CONV):
            off = SUBLANES - (FFN_CONV - 1) + t
            cv = cv + ub_ref[slot, off:off + tm, :] * cw_ref[j, t:t + 1, :]
        act = _silu(cv[:, :FFN_CHUNK]) * cv[:, FFN_CHUNK:]
        acc = acc + _dot(act.astype(BF16), wd_ref[j * FFN_CHUNK:(j + 1) * FFN_CHUNK, :])
    y = x + acc
    if final_norm:
        y = y * lax.rsqrt(jnp.mean(y * y, axis=-1, keepdims=True) + EPS) * fw_ref[...]
    o_ref[...] = y


def _ffn(x2, norm_w, w_up, conv_w, conv_b, w_down, final_w, batch, seq, tm, final_norm):
    nt = seq // tm
    row = lambda b, t: (b * nt + t, 0)
    c2 = lambda b, t: (0, 0)
    c3 = lambda b, t: (0, 0, 0)
    once = pl.Buffered(1)
    return pl.pallas_call(
        functools.partial(_ffn_kernel, tm=tm, final_norm=final_norm),
        grid=(batch, nt),
        in_specs=[pl.BlockSpec((tm, D_MODEL), row),
                  pl.BlockSpec((1, D_MODEL), c2),
                  pl.BlockSpec((FFN_NCHUNK, D_MODEL, 2 * FFN_CHUNK), c3, pipeline_mode=once),
                  pl.BlockSpec((FFN_NCHUNK, SUBLANES, 2 * FFN_CHUNK), c3),
                  pl.BlockSpec((FFN_NCHUNK, 1, 2 * FFN_CHUNK), c3),
                  pl.BlockSpec((D_FF, D_MODEL), c2, pipeline_mode=once),
                  pl.BlockSpec((1, D_MODEL), c2)],
        out_specs=pl.BlockSpec((tm, D_MODEL), row),
        out_shape=jax.ShapeDtypeStruct(x2.shape, F32),
        scratch_shapes=[pltpu.VMEM((2, tm + SUBLANES, 2 * FFN_CHUNK), F32),
                        pltpu.VMEM((FFN_NCHUNK, SUBLANES, 2 * FFN_CHUNK), F32)],
        compiler_params=_params(("parallel", "arbitrary")),
        name="ffn",
    )(x2, norm_w, w_up, conv_w, conv_b, w_down, final_w)


def _ffn_chunked(t):
    lead = t.shape[:-1]
    t = t.reshape(lead + (2, FFN_NCHUNK, FFN_CHUNK))
    t = jnp.moveaxis(t, -2, 0)
    return t.reshape((FFN_NCHUNK,) + lead + (2 * FFN_CHUNK,))


def kernel(x, attn_norm_w, w_in, gdn_conv_w, gdn_a_log, gdn_dt_bias, gdn_norm_w, s5_lambda_re, s5_lambda_im, s5_log_dt, s5_b_re, s5_b_im, s5_c_re, s5_c_im, s5_d, s5_w_glu, s5_b_glu, s5_norm_w, diff_lambda_q1, diff_lambda_k1, diff_lambda_q2, diff_lambda_k2, diff_norm_w, w_out, ffn_norm_w, ffn_w_up, ffn_conv_w, ffn_conv_b, ffn_w_down, final_norm_w):
    batch, seq, _ = x.shape
    depth = w_in.shape[0]
    tm = min(512, seq)
    tl = min(256, seq)
    tq = min(256, seq)
    ts = min(32, seq)
    assert seq % tm == 0 and seq % tl == 0 and seq % tq == 0 and seq % ts == 0
    assert tl % GDN_CHUNK == 0 and batch % SUBLANES == 0

    x2 = x.reshape(batch * seq, D_MODEL).astype(F32)
    o_qkv, o_z, o_b, o_a, o_su = 0, 3 * GDN_WIDTH, 4 * GDN_WIDTH, 4 * GDN_WIDTH + GDN_HEADS, 4 * GDN_WIDTH + 2 * GDN_HEADS
    o_dq = o_su + S5_WIDTH
    for i in range(depth):
        wi = w_in[i]
        pad = jnp.zeros((D_MODEL, LANES - 2 * GDN_HEADS), wi.dtype)
        wf = jnp.concatenate([wi[:, o_qkv:o_b], wi[:, o_b:o_su], pad], axis=1).astype(BF16)
        ws = wi[:, o_su:o_dq].astype(BF16)
        wa = wi[:, o_dq:].astype(BF16)
        inf, s_u, dqkv = _inproj(x2, attn_norm_w[i][None, :], wf, ws, wa, batch, seq, tm)

        gate_p = jnp.zeros((SUBLANES, LANES), F32)
        gate_p = gate_p.at[0, GDN_HEADS:2 * GDN_HEADS].set(gdn_a_log[i].astype(F32))
        gate_p = gate_p.at[1, GDN_HEADS:2 * GDN_HEADS].set(gdn_dt_bias[i].astype(F32))
        o_gdn = _gdn(inf, gdn_conv_w[i], gate_p, gdn_norm_w[i][None, :].astype(F32), batch, seq, tl)

        a_bar, bre, bim, cre, cim = _s5_operators(s5_lambda_re[i], s5_lambda_im[i], s5_log_dt[i],
                                                  s5_b_re[i], s5_b_im[i], s5_c_re[i], s5_c_im[i])
        o_s5 = _s5(s_u.reshape(batch, seq, S5_WIDTH), a_bar, bre, bim, cre, cim,
                   s5_d[i][None, :], s5_w_glu[i].astype(BF16), s5_b_glu[i][None, :],
                   s5_norm_w[i][None, :], batch, seq, ts)

        lam_init = 0.8 - 0.6 * math.exp(-0.3 * i)
        lam_p = jnp.zeros((SUBLANES, LANES), F32)
        for r, t in enumerate((diff_lambda_q1, diff_lambda_k1, diff_lambda_q2, diff_lambda_k2)):
            lam_p = lam_p.at[r, :DIFF_DIM].set(t[i].astype(F32))
        o_diff = _diff_attn(dqkv, lam_p, diff_norm_w[i][None, :], batch, seq, tq, lam_init)

        x2 = _outproj(x2, o_gdn, o_s5.reshape(batch * seq, S5_WIDTH), o_diff, w_out[i].astype(BF16),
                      batch, seq, tm)

        cw = jnp.zeros((FFN_NCHUNK, SUBLANES, 2 * FFN_CHUNK), F32)
        cw = cw.at[:, :FFN_CONV, :].set(_ffn_chunked(ffn_conv_w[i]))
        x2 = _ffn(x2, ffn_norm_w[i][None, :], _ffn_chunked(ffn_w_up[i]).astype(BF16), cw,
                  _ffn_chunked(ffn_conv_b[i][None, :]), ffn_w_down[i].astype(BF16),
                  final_norm_w[None, :], batch, seq, tm, final_norm=(i == depth - 1))
    return x2.reshape(batch, seq, D_MODEL)
```

```python
import functools
import math

import jax
import jax.numpy as jnp
from jax import lax
from jax.experimental import pallas as pl
from jax.experimental.pallas import tpu as pltpu

F32 = jnp.float32
BF16 = jnp.bfloat16
HIGHEST = lax.Precision.HIGHEST

D_MODEL = 1024
EPS = 1e-6
GDN_HEADS = 4
GDN_DIM = 128
GDN_WIDTH = GDN_HEADS * GDN_DIM
GDN_CONV = 4
GDN_CHUNK = 64
GDN_INV_BLOCK = 16
S5_GROUP = 16
S5_GROUPS = 32
S5_WIDTH = S5_GROUPS * S5_GROUP
S5_STATE = 64
S5_NSTATE = S5_GROUPS * S5_STATE
S5_HALF_IN = S5_WIDTH // 2
S5_HALF_ST = S5_NSTATE // 2
DIFF_HEADS = 4
DIFF_DIM = 64
DIFF_V = 2 * DIFF_DIM
DIFF_WIDTH = DIFF_HEADS * DIFF_V
MIX_WIDTH = GDN_WIDTH + S5_WIDTH + DIFF_WIDTH
D_FF = 2816
FFN_CONV = 3
FFN_CHUNK = 256
FFN_NCHUNK = D_FF // FFN_CHUNK
FFN_AHEAD = 2

LANES = 128
SUBLANES = 8
VMEM_LIMIT = 56 * 1024 * 1024

INF_QKV = 3 * GDN_WIDTH
INF_WIDTH = INF_QKV + GDN_WIDTH + LANES


def _dot(a, b):
    return jnp.dot(a, b, preferred_element_type=F32)


def _dot_nt(a, b, precision=None):
    return lax.dot_general(a, b, (((1,), (1,)), ((), ())), precision=precision,
                           preferred_element_type=F32)


def _dot_tn(a, b):
    return lax.dot_general(a, b, (((0,), (0,)), ((), ())), preferred_element_type=F32)


def _sigmoid(x):
    return 1.0 / (1.0 + jnp.exp(-x))


def _silu(x):
    return x * _sigmoid(x)


def _params(sem):
    return pltpu.CompilerParams(dimension_semantics=sem, vmem_limit_bytes=VMEM_LIMIT)


def _inproj_kernel(x_ref, nw_ref, wf_ref, ws_ref, wa_ref, of_ref, os_ref, oa_ref):
    x = x_ref[...]
    ms = jnp.mean(x * x, axis=-1, keepdims=True)
    h = ((x * lax.rsqrt(ms + EPS)) * nw_ref[...]).astype(BF16)
    of_ref[...] = _dot(h, wf_ref[...])
    os_ref[...] = _dot(h, ws_ref[...])
    oa_ref[...] = _dot(h, wa_ref[...]).astype(BF16)


def _inproj(x2, nw, wf, ws, wa, batch, seq, tm):
    nt = seq // tm
    tokens = batch * seq
    row = lambda b, t: (b * nt + t, 0)
    const = lambda b, t: (0, 0)
    return pl.pallas_call(
        _inproj_kernel,
        grid=(batch, nt),
        in_specs=[pl.BlockSpec((tm, D_MODEL), row),
                  pl.BlockSpec((1, D_MODEL), const),
                  pl.BlockSpec((D_MODEL, INF_WIDTH), const),
                  pl.BlockSpec((D_MODEL, S5_WIDTH), const),
                  pl.BlockSpec((D_MODEL, 3 * DIFF_WIDTH), const)],
        out_specs=[pl.BlockSpec((tm, INF_WIDTH), row),
                   pl.BlockSpec((tm, S5_WIDTH), row),
                   pl.BlockSpec((tm, 3 * DIFF_WIDTH), row)],
        out_shape=[jax.ShapeDtypeStruct((tokens, INF_WIDTH), F32),
                   jax.ShapeDtypeStruct((tokens, S5_WIDTH), F32),
                   jax.ShapeDtypeStruct((tokens, 3 * DIFF_WIDTH), BF16)],
        compiler_params=_params(("parallel", "parallel")),
        name="inproj",
    )(x2, nw, wf, ws, wa)


def _unit_lower_inverse(mats, row, col):
    shift = int(math.log2(GDN_INV_BLOCK))
    same = (row >> shift) == (col >> shift)
    eye = (row == col).astype(F32)
    mm = lambda ps, qs: [_dot(p.astype(BF16), q.astype(BF16)) for p, q in zip(ps, qs)]
    add = lambda ps, qs: [p + q for p, q in zip(ps, qs)]
    ad = [jnp.where(same, a, 0.0) for a in mats]
    ao = [jnp.where(same, 0.0, a) for a in mats]
    p = [eye - a for a in ad]
    x = mm(ad, ad)
    p = add(p, mm(p, x))
    x = mm(x, x)
    p = add(p, mm(p, x))
    x = mm(x, x)
    td = add(p, mm(p, x))
    n1 = mm(td, ao)
    n2 = mm(n1, n1)
    n3 = mm(n1, n2)
    return mm([eye - a + b - d for a, b, d in zip(n1, n2, n3)], td)


def _gdn_kernel(qkv_ref, z_ref, ba_ref, cw_ref, gp_ref, nw_ref, o_ref, xs_ref, q_s, k_s, v_s, st_ref,
                *, tl):
    c = GDN_CHUNK
    ti = pl.program_id(1)

    @pl.when(ti == 0)
    def _():
        xs_ref[0:SUBLANES, :] = jnp.zeros((SUBLANES, INF_QKV), F32)
        st_ref[...] = jnp.zeros_like(st_ref)

    xs_ref[SUBLANES:SUBLANES + tl, :] = qkv_ref[...]
    for h in range(GDN_HEADS):
        for part, dst in enumerate((q_s, k_s, v_s)):
            lo = part * GDN_WIDTH + h * GDN_DIM
            acc = jnp.zeros((tl, GDN_DIM), F32)
            for j in range(GDN_CONV):
                off = SUBLANES - (GDN_CONV - 1) + j
                acc = acc + xs_ref[off:off + tl, lo:lo + GDN_DIM] * cw_ref[j:j + 1, lo:lo + GDN_DIM]
            y = _silu(acc)
            if part < 2:
                y = y * lax.rsqrt(jnp.sum(y * y, axis=-1, keepdims=True) + 1e-6)
            if part == 0:
                y = y * (GDN_DIM ** -0.5)
            dst[h] = y
    xs_ref[0:SUBLANES, :] = xs_ref[tl:tl + SUBLANES, :]

    ba = ba_ref[...]
    beta_all = _sigmoid(ba)
    sp = ba + gp_ref[1:2, :]
    softplus = jnp.maximum(sp, 0.0) + jnp.log(1.0 + jnp.exp(-jnp.abs(sp)))
    g_all = -jnp.exp(gp_ref[0:1, :]) * softplus

    row = lax.broadcasted_iota(jnp.int32, (c, c), 0)
    col = lax.broadcasted_iota(jnp.int32, (c, c), 1)
    incl = row >= col
    strict = row > col
    tri = incl.astype(F32)
    sel_r = lax.broadcasted_iota(jnp.int32, (SUBLANES, LANES), 0)
    sel_c = lax.broadcasted_iota(jnp.int32, (SUBLANES, LANES), 1)
    sel = (sel_c == sel_r + GDN_HEADS).astype(F32)

    nc = tl // c
    nh = GDN_HEADS
    units = [(ci, h) for ci in range(nc) for h in range(nh)]

    def rows(ci):
        return slice(ci * c, (ci + 1) * c)

    gcol = [jnp.dot(tri, g_all[rows(ci), :], precision=HIGHEST, preferred_element_type=F32)
            for ci in range(nc)]
    grow = [_dot_nt(sel, gcol[ci], precision=HIGHEST) for ci in range(nc)]

    q = [q_s[h, rows(ci), :] for ci, h in units]
    k = [k_s[h, rows(ci), :] for ci, h in units]
    v = [v_s[h, rows(ci), :] for ci, h in units]
    beta = [beta_all[rows(ci), h:h + 1] for ci, h in units]
    gc = [gcol[ci][:, nh + h:nh + h + 1] for ci, h in units]
    gr = [grow[ci][h:h + 1, :] for ci, h in units]
    decay = [jnp.where(incl, jnp.exp(jnp.where(incl, a - b, 0.0)), 0.0) for a, b in zip(gc, gr)]
    egc = [jnp.exp(a) for a in gc]
    kb = [a * b for a, b in zip(k, beta)]
    kbf = [a.astype(BF16) for a in k]
    kk = [_dot_nt(a.astype(BF16), b) for a, b in zip(kb, kbf)]
    qk = [_dot_nt(a.astype(BF16), b) for a, b in zip(q, kbf)]
    a_low = [jnp.where(strict, a * d, 0.0) for a, d in zip(kk, decay)]
    attn = [jnp.where(incl, a * d, 0.0).astype(BF16) for a, d in zip(qk, decay)]
    t_inv = [t.astype(BF16) for t in _unit_lower_inverse(a_low, row, col)]
    u_mat = [_dot(t, (a * b).astype(BF16)) for t, a, b in zip(t_inv, v, beta)]
    w_mat = [_dot(t, (a * e).astype(BF16)).astype(BF16) for t, a, e in zip(t_inv, kb, egc)]
    qg = [(a * e).astype(BF16) for a, e in zip(q, egc)]
    g_last = [a[c - 1:c, :] for a in gc]
    kd = [(a * jnp.exp(gl - g)).astype(BF16) for a, gl, g in zip(k, g_last, gc)]
    eg_last = [jnp.exp(gl) for gl in g_last]

    state = [st_ref[h] for h in range(nh)]
    for ci in range(nc):
        us = [ci * nh + h for h in range(nh)]
        sb = [s.astype(BF16) for s in state]
        ws = [_dot(w_mat[u], sb[h]) for h, u in enumerate(us)]
        qs = [_dot(qg[u], sb[h]) for h, u in enumerate(us)]
        v_new = [(u_mat[u] - ws[h]).astype(BF16) for h, u in enumerate(us)]
        o = [qs[h] + _dot(attn[u], v_new[h]) for h, u in enumerate(us)]
        state = [state[h] * eg_last[u] + _dot_tn(kd[u], v_new[h]) for h, u in enumerate(us)]
        for h in range(nh):
            on = o[h] * lax.rsqrt(jnp.mean(o[h] * o[h], axis=-1, keepdims=True) + EPS) * nw_ref[...]
            zz = z_ref[rows(ci), h * GDN_DIM:(h + 1) * GDN_DIM]
            o_ref[rows(ci), h * GDN_DIM:(h + 1) * GDN_DIM] = (on * _silu(zz)).astype(BF16)
    for h in range(nh):
        st_ref[h] = state[h]


def _gdn(inf, conv_w, gate_p, norm_w, batch, seq, tl):
    nt = seq // tl
    tokens = batch * seq
    row = lambda b, t: (b * nt + t, 0)
    const = lambda b, t: (0, 0)
    return pl.pallas_call(
        functools.partial(_gdn_kernel, tl=tl),
        grid=(batch, nt),
        in_specs=[pl.BlockSpec((tl, INF_QKV), row),
                  pl.BlockSpec((tl, GDN_WIDTH), lambda b, t: (b * nt + t, INF_QKV // GDN_WIDTH)),
                  pl.BlockSpec((tl, LANES), lambda b, t: (b * nt + t, (INF_QKV + GDN_WIDTH) // LANES)),
                  pl.BlockSpec((GDN_CONV, INF_QKV), const),
                  pl.BlockSpec((SUBLANES, LANES), const),
                  pl.BlockSpec((1, GDN_DIM), const)],
        out_specs=pl.BlockSpec((tl, GDN_WIDTH), row),
        out_shape=jax.ShapeDtypeStruct((tokens, GDN_WIDTH), BF16),
        scratch_shapes=[pltpu.VMEM((tl + SUBLANES, INF_QKV), F32),
                        pltpu.VMEM((GDN_HEADS, tl, GDN_DIM), F32),
                        pltpu.VMEM((GDN_HEADS, tl, GDN_DIM), F32),
                        pltpu.VMEM((GDN_HEADS, tl, GDN_DIM), F32),
                        pltpu.VMEM((GDN_HEADS, GDN_DIM, GDN_DIM), F32)],
        compiler_params=_params(("parallel", "arbitrary")),
        name="gdn",
    )(inf, inf, inf, conv_w, gate_p, norm_w)


S5_SCAN_LANES = 512


def _gelu_tanh(x):
    return 0.5 * x * (1.0 + jnp.tanh(math.sqrt(2.0 / math.pi) * (x + 0.044715 * (x * x * x))))


def _s5_kernel(u_ref, a_ref, bre_ref, bim_ref, cre_ref, cim_ref, d_ref, wg_ref, bg_ref, nw_ref, o_ref,
               sre_ref, sim_ref, *, batch, tl):
    rows = batch * tl

    @pl.when(pl.program_id(0) == 0)
    def _():
        sre_ref[0:batch, :] = jnp.zeros((batch, S5_NSTATE), F32)
        sim_ref[0:batch, :] = jnp.zeros((batch, S5_NSTATE), F32)

    u_bm = u_ref[...].reshape(rows, S5_WIDTH)
    r_i = lax.broadcasted_iota(jnp.int32, (rows, rows), 0)
    c_i = lax.broadcasted_iota(jnp.int32, (rows, rows), 1)
    to_tm = (c_i == (r_i % batch) * tl + r_i // batch).astype(BF16)
    to_bm = (c_i == (r_i % tl) * batch + r_i // tl).astype(BF16)
    u_hi = u_bm.astype(BF16)
    u_lo = (u_bm - u_hi.astype(F32)).astype(BF16)
    u_hi_tm = _dot(to_tm, u_hi)
    u = u_hi_tm + _dot(to_tm, u_lo)
    ub = u_hi_tm.astype(BF16)
    for hf in range(2):
        cin = slice(hf * S5_HALF_IN, (hf + 1) * S5_HALF_IN)
        cst = slice(hf * S5_HALF_ST, (hf + 1) * S5_HALF_ST)
        sre_ref[batch:batch + rows, cst] = _dot(ub[:, cin], bre_ref[hf])
        sim_ref[batch:batch + rows, cst] = _dot(ub[:, cin], bim_ref[hf])

    for lc in range(S5_NSTATE // S5_SCAN_LANES):
        cols = slice(lc * S5_SCAN_LANES, (lc + 1) * S5_SCAN_LANES)
        a_re = jnp.broadcast_to(a_ref[0:1, cols], (batch, S5_SCAN_LANES))
        a_im = jnp.broadcast_to(a_ref[1:2, cols], (batch, S5_SCAN_LANES))

        def step(t, carry):
            s_re, s_im = carry
            r = pl.ds(pl.multiple_of((t + 1) * batch, batch), batch)
            n_re = a_re * s_re - a_im * s_im + sre_ref[r, cols]
            n_im = a_re * s_im + a_im * s_re + sim_ref[r, cols]
            sre_ref[r, cols] = n_re
            sim_ref[r, cols] = n_im
            return n_re, n_im

        s_re, s_im = lax.fori_loop(0, tl, step, (sre_ref[0:batch, cols], sim_ref[0:batch, cols]))
        sre_ref[0:batch, cols] = s_re
        sim_ref[0:batch, cols] = s_im

    ys = []
    for hf in range(2):
        cst = slice(hf * S5_HALF_ST, (hf + 1) * S5_HALF_ST)
        s_re = sre_ref[batch:batch + rows, cst].astype(BF16)
        s_im = sim_ref[batch:batch + rows, cst].astype(BF16)
        ys.append(_dot(s_re, cre_ref[hf]) - _dot(s_im, cim_ref[hf]))
    y = jnp.concatenate(ys, axis=-1) + d_ref[...] * u
    y = _gelu_tanh(y)
    y = y * _sigmoid(_dot(y.astype(BF16), wg_ref[...]) + bg_ref[...])
    y = (y * lax.rsqrt(jnp.mean(y * y, axis=-1, keepdims=True) + EPS) * nw_ref[...]).astype(BF16)
    o_ref[...] = _dot(to_bm, y).astype(BF16).reshape(batch, tl, S5_WIDTH)


def _s5(u3, a_bar, bre, bim, cre, cim, d_skip, w_glu, b_glu, norm_w, batch, seq, tl):
    rows = batch * tl
    blk = lambda t: (0, t, 0)
    c2 = lambda t: (0, 0)
    c3 = lambda t: (0, 0, 0)
    return pl.pallas_call(
        functools.partial(_s5_kernel, batch=batch, tl=tl),
        grid=(seq // tl,),
        in_specs=[pl.BlockSpec((batch, tl, S5_WIDTH), blk),
                  pl.BlockSpec((SUBLANES, S5_NSTATE), c2),
                  pl.BlockSpec((2, S5_HALF_IN, S5_HALF_ST), c3),
                  pl.BlockSpec((2, S5_HALF_IN, S5_HALF_ST), c3),
                  pl.BlockSpec((2, S5_HALF_ST, S5_HALF_IN), c3),
                  pl.BlockSpec((2, S5_HALF_ST, S5_HALF_IN), c3),
                  pl.BlockSpec((1, S5_WIDTH), c2),
                  pl.BlockSpec((S5_WIDTH, S5_WIDTH), c2),
                  pl.BlockSpec((1, S5_WIDTH), c2),
                  pl.BlockSpec((1, S5_WIDTH), c2)],
        out_specs=pl.BlockSpec((batch, tl, S5_WIDTH), blk),
        out_shape=jax.ShapeDtypeStruct((batch, seq, S5_WIDTH), BF16),
        scratch_shapes=[pltpu.VMEM((batch + rows, S5_NSTATE), F32),
                        pltpu.VMEM((batch + rows, S5_NSTATE), F32)],
        compiler_params=_params(("arbitrary",)),
        name="s5",
    )(u3, a_bar, bre, bim, cre, cim, d_skip, w_glu, b_glu, norm_w)


def _s5_operators(lam_re, lam_im, log_dt, b_re, b_im, c_re, c_im):
    l_re = jnp.minimum(lam_re.astype(F32), -1e-4)
    l_im = lam_im.astype(F32)
    dt = jnp.exp(log_dt.astype(F32))[:, None]
    mag = jnp.exp(l_re * dt)
    a_re = mag * jnp.cos(l_im * dt)
    a_im = mag * jnp.sin(l_im * dt)
    den = l_re * l_re + l_im * l_im
    f_re = ((a_re - 1.0) * l_re + a_im * l_im) / den
    f_im = (a_im * l_re - (a_re - 1.0) * l_im) / den
    bb_re = f_re[..., None] * b_re.astype(F32) - f_im[..., None] * b_im.astype(F32)
    bb_im = f_re[..., None] * b_im.astype(F32) + f_im[..., None] * b_re.astype(F32)
    a_bar = jnp.zeros((SUBLANES, S5_NSTATE), F32)
    a_bar = a_bar.at[0].set(a_re.reshape(-1)).at[1].set(a_im.reshape(-1))
    eye = jnp.eye(S5_GROUPS, dtype=F32)

    def b_op(t):
        full = jnp.einsum('gph,gk->ghkp', t, eye).reshape(S5_WIDTH, S5_NSTATE)
        return jnp.stack([full[:S5_HALF_IN, :S5_HALF_ST], full[S5_HALF_IN:, S5_HALF_ST:]]).astype(BF16)

    def c_op(t):
        full = jnp.einsum('ghp,gk->gpkh', t, eye).reshape(S5_NSTATE, S5_WIDTH)
        return jnp.stack([full[:S5_HALF_ST, :S5_HALF_IN], full[S5_HALF_ST:, S5_HALF_IN:]]).astype(BF16)

    return a_bar, b_op(bb_re), b_op(bb_im), c_op(c_re.astype(F32)), c_op(c_im.astype(F32))


ATT_ROWS = 32


def _diff_kernel(q_ref, k_ref, v_ref, lp_ref, nw_ref, o_ref, m_ref, l_ref, acc_ref, *, tq, lam_init):
    qi = pl.program_id(1)
    nh = DIFF_HEADS
    lane = lax.broadcasted_iota(jnp.int32, (tq, DIFF_V), 1)
    scale = DIFF_DIM ** -0.5 * math.log2(math.e)
    qs = []
    for h in range(nh):
        q = q_ref[:, h * DIFF_V:(h + 1) * DIFF_V]
        zero = jnp.zeros_like(q)
        q0 = jnp.where(lane < DIFF_DIM, q, zero)
        q1 = jnp.where(lane < DIFF_DIM, zero, q)
        qs.append((jnp.concatenate([q0, q1], axis=0).astype(F32) * scale).astype(BF16))

    m_ref[...] = jnp.full_like(m_ref, -jnp.inf)
    l_ref[...] = jnp.zeros_like(l_ref)
    acc_ref[...] = jnp.zeros_like(acc_ref)

    def update(kb, masked):
        r = pl.ds(pl.multiple_of(kb * tq, tq), tq)
        hs = range(nh)
        kk = [k_ref[r, h * DIFF_V:(h + 1) * DIFF_V] for h in hs]
        vv = [v_ref[r, h * DIFF_V:(h + 1) * DIFF_V] for h in hs]
        s = {0: _dot_nt(qs[0], kk[0])}
        for h in hs:
            if h + 1 < nh:
                s[h + 1] = _dot_nt(qs[h + 1], kk[h + 1])
            pg, ag = [], []
            for g in range(2 * tq // ATT_ROWS):
                r0 = g * ATT_ROWS
                sg = s[h][r0:r0 + ATT_ROWS, :]
                if masked:
                    rr = lax.broadcasted_iota(jnp.int32, (ATT_ROWS, tq), 0) + (r0 % tq)
                    cc = lax.broadcasted_iota(jnp.int32, (ATT_ROWS, tq), 1)
                    sg = jnp.where(rr >= cc, sg, -jnp.inf)
                tiles = [sg[:, j * LANES:(j + 1) * LANES] for j in range(tq // LANES)]
                m_cur = jnp.max(functools.reduce(jnp.maximum, tiles), axis=-1, keepdims=True)
                m_prev = m_ref[h, r0:r0 + ATT_ROWS, :]
                m_new = jnp.maximum(m_prev, m_cur)
                a = jnp.exp2(m_prev - m_new)
                p = [jnp.exp2(t - m_new) for t in tiles]
                l_ref[h, r0:r0 + ATT_ROWS, :] = a * l_ref[h, r0:r0 + ATT_ROWS, :] + functools.reduce(jnp.add, p)
                m_ref[h, r0:r0 + ATT_ROWS, :] = m_new
                pg.append(jnp.concatenate([t.astype(BF16) for t in p], axis=-1))
                ag.append(a)
            pv = _dot(jnp.concatenate(pg, axis=0), vv[h])
            acc_ref[h] = jnp.concatenate(ag, axis=0) * acc_ref[h] + pv

    def body(kb, carry):
        update(kb, False)
        return carry

    lax.fori_loop(0, qi, body, 0)
    update(qi, True)

    lp = lp_ref[...]
    lam = (jnp.exp(jnp.sum(lp[0:1, :] * lp[1:2, :], axis=-1, keepdims=True))
           - jnp.exp(jnp.sum(lp[2:3, :] * lp[3:4, :], axis=-1, keepdims=True)) + lam_init)
    for h in range(nh):
        o = acc_ref[h] / jnp.sum(l_ref[h], axis=-1, keepdims=True)
        o = o[:tq, :] - lam * o[tq:, :]
        on = o * lax.rsqrt(jnp.mean(o * o, axis=-1, keepdims=True) + EPS) * nw_ref[...]
        o_ref[:, h * DIFF_V:(h + 1) * DIFF_V] = (on * (1.0 - lam_init)).astype(BF16)


def _diff_attn(qkv, lam_p, norm_w, batch, seq, tq, lam_init):
    nq = seq // tq
    tokens = batch * seq
    return pl.pallas_call(
        functools.partial(_diff_kernel, tq=tq, lam_init=lam_init),
        grid=(batch, nq),
        in_specs=[pl.BlockSpec((tq, DIFF_WIDTH), lambda b, i: (b * nq + i, 0)),
                  pl.BlockSpec((seq, DIFF_WIDTH), lambda b, i: (b, 1)),
                  pl.BlockSpec((seq, DIFF_WIDTH), lambda b, i: (b, 2)),
                  pl.BlockSpec((SUBLANES, LANES), lambda b, i: (0, 0)),
                  pl.BlockSpec((1, DIFF_V), lambda b, i: (0, 0))],
        out_specs=pl.BlockSpec((tq, DIFF_WIDTH), lambda b, i: (b * nq + i, 0)),
        out_shape=jax.ShapeDtypeStruct((tokens, DIFF_WIDTH), BF16),
        scratch_shapes=[pltpu.VMEM((DIFF_HEADS, 2 * tq, LANES), F32),
                        pltpu.VMEM((DIFF_HEADS, 2 * tq, LANES), F32),
                        pltpu.VMEM((DIFF_HEADS, 2 * tq, DIFF_V), F32)],
        compiler_params=_params(("parallel", "parallel")),
        name="diff_attn",
    )(qkv, qkv, qkv, lam_p, norm_w)


def _outproj_kernel(x_ref, og_ref, os_ref, od_ref, w_ref, o_ref):
    acc = _dot(og_ref[...], w_ref[0:GDN_WIDTH, :])
    acc = acc + _dot(os_ref[...], w_ref[GDN_WIDTH:GDN_WIDTH + S5_WIDTH, :])
    acc = acc + _dot(od_ref[...], w_ref[GDN_WIDTH + S5_WIDTH:MIX_WIDTH, :])
    o_ref[...] = x_ref[...] + acc


def _outproj(x2, o_gdn, o_s5, o_diff, w_out, batch, seq, tm):
    nt = seq // tm
    row = lambda b, t: (b * nt + t, 0)
    return pl.pallas_call(
        _outproj_kernel,
        grid=(batch, nt),
        in_specs=[pl.BlockSpec((tm, D_MODEL), row),
                  pl.BlockSpec((tm, GDN_WIDTH), row),
                  pl.BlockSpec((tm, S5_WIDTH), row),
                  pl.BlockSpec((tm, DIFF_WIDTH), row),
                  pl.BlockSpec((MIX_WIDTH, D_MODEL), lambda b, t: (0, 0))],
        out_specs=pl.BlockSpec((tm, D_MODEL), row),
        out_shape=jax.ShapeDtypeStruct(x2.shape, F32),
        compiler_params=_params(("parallel", "parallel")),
        name="outproj",
    )(x2, o_gdn, o_s5, o_diff, w_out)


def _ffn_kernel(x_ref, nw_ref, wu_ref, cw_ref, cb_ref, wd_ref, fw_ref, o_ref, ub_ref, tail_ref, act_ref,
                *, tm, final_norm):
    @pl.when(pl.program_id(1) == 0)
    def _():
        tail_ref[...] = jnp.zeros_like(tail_ref)

    x = x_ref[...]
    ms = jnp.mean(x * x, axis=-1, keepdims=True)
    h = ((x * lax.rsqrt(ms + EPS)) * nw_ref[...]).astype(BF16)
    nslot = FFN_AHEAD + 1
    for j in range(min(FFN_AHEAD, FFN_NCHUNK)):
        ub_ref[j, SUBLANES:SUBLANES + tm, :] = _dot(h, wu_ref[j])
    for j in range(FFN_NCHUNK):
        slot = j % nslot
        if j + FFN_AHEAD < FFN_NCHUNK:
            ub_ref[(j + FFN_AHEAD) % nslot, SUBLANES:SUBLANES + tm, :] = _dot(h, wu_ref[j + FFN_AHEAD])
        ub_ref[slot, 0:SUBLANES, :] = tail_ref[j]
        tail_ref[j] = ub_ref[slot, tm:tm + SUBLANES, :]
        cv = cb_ref[j]
        for t in range(FFN_CONV):
            off = SUBLANES - (FFN_CONV - 1) + t
            cv = cv + ub_ref[slot, off:off + tm, :] * cw_ref[j, t:t + 1, :]
        act = _silu(cv[:, :FFN_CHUNK]) * cv[:, FFN_CHUNK:]
        act_ref[:, j * FFN_CHUNK:(j + 1) * FFN_CHUNK] = act.astype(BF16)
    y = x + _dot(act_ref[...], wd_ref[...])
    if final_norm:
        y = y * lax.rsqrt(jnp.mean(y * y, axis=-1, keepdims=True) + EPS) * fw_ref[...]
    o_ref[...] = y


def _ffn(x2, norm_w, w_up, conv_w, conv_b, w_down, final_w, batch, seq, tm, final_norm):
    nt = seq // tm
    row = lambda b, t: (b * nt + t, 0)
    c2 = lambda b, t: (0, 0)
    c3 = lambda b, t: (0, 0, 0)
    once = pl.Buffered(1)
    return pl.pallas_call(
        functools.partial(_ffn_kernel, tm=tm, final_norm=final_norm),
        grid=(batch, nt),
        in_specs=[pl.BlockSpec((tm, D_MODEL), row),
                  pl.BlockSpec((1, D_MODEL), c2),
                  pl.BlockSpec((FFN_NCHUNK, D_MODEL, 2 * FFN_CHUNK), c3, pipeline_mode=once),
                  pl.BlockSpec((FFN_NCHUNK, SUBLANES, 2 * FFN_CHUNK), c3),
                  pl.BlockSpec((FFN_NCHUNK, 1, 2 * FFN_CHUNK), c3),
                  pl.BlockSpec((D_FF, D_MODEL), c2, pipeline_mode=once),
                  pl.BlockSpec((1, D_MODEL), c2)],
        out_specs=pl.BlockSpec((tm, D_MODEL), row),
        out_shape=jax.ShapeDtypeStruct(x2.shape, F32),
        scratch_shapes=[pltpu.VMEM((FFN_AHEAD + 1, tm + SUBLANES, 2 * FFN_CHUNK), F32),
                        pltpu.VMEM((FFN_NCHUNK, SUBLANES, 2 * FFN_CHUNK), F32),
                        pltpu.VMEM((tm, D_FF), BF16)],
        compiler_params=_params(("parallel", "arbitrary")),
        name="ffn",
    )(x2, norm_w, w_up, conv_w, conv_b, w_down, final_w)


def _ffn_chunked(t):
    lead = t.shape[:-1]
    t = t.reshape(lead + (2, FFN_NCHUNK, FFN_CHUNK))
    t = jnp.moveaxis(t, -2, 0)
    return t.reshape((FFN_NCHUNK,) + lead + (2 * FFN_CHUNK,))


def kernel(x, attn_norm_w, w_in, gdn_conv_w, gdn_a_log, gdn_dt_bias, gdn_norm_w, s5_lambda_re, s5_lambda_im, s5_log_dt, s5_b_re, s5_b_im, s5_c_re, s5_c_im, s5_d, s5_w_glu, s5_b_glu, s5_norm_w, diff_lambda_q1, diff_lambda_k1, diff_lambda_q2, diff_lambda_k2, diff_norm_w, w_out, ffn_norm_w, ffn_w_up, ffn_conv_w, ffn_conv_b, ffn_w_down, final_norm_w):
    batch, seq, _ = x.shape
    depth = w_in.shape[0]
    tm = min(512, seq)
    tl = min(256, seq)
    tq = min(256, seq)
    ts = min(32, seq)
    assert seq % tm == 0 and seq % tl == 0 and seq % tq == 0 and seq % ts == 0
    assert tl % GDN_CHUNK == 0 and batch % SUBLANES == 0

    x2 = x.reshape(batch * seq, D_MODEL).astype(F32)
    o_qkv, o_z, o_b, o_a, o_su = 0, 3 * GDN_WIDTH, 4 * GDN_WIDTH, 4 * GDN_WIDTH + GDN_HEADS, 4 * GDN_WIDTH + 2 * GDN_HEADS
    o_dq = o_su + S5_WIDTH
    for i in range(depth):
        wi = w_in[i]
        pad = jnp.zeros((D_MODEL, LANES - 2 * GDN_HEADS), wi.dtype)
        wf = jnp.concatenate([wi[:, o_qkv:o_b], wi[:, o_b:o_su], pad], axis=1).astype(BF16)
        ws = wi[:, o_su:o_dq].astype(BF16)
        wa = wi[:, o_dq:].astype(BF16)
        inf, s_u, dqkv = _inproj(x2, attn_norm_w[i][None, :], wf, ws, wa, batch, seq, tm)

        gate_p = jnp.zeros((SUBLANES, LANES), F32)
        gate_p = gate_p.at[0, GDN_HEADS:2 * GDN_HEADS].set(gdn_a_log[i].astype(F32))
        gate_p = gate_p.at[1, GDN_HEADS:2 * GDN_HEADS].set(gdn_dt_bias[i].astype(F32))
        o_gdn = _gdn(inf, gdn_conv_w[i], gate_p, gdn_norm_w[i][None, :].astype(F32), batch, seq, tl)

        a_bar, bre, bim, cre, cim = _s5_operators(s5_lambda_re[i], s5_lambda_im[i], s5_log_dt[i],
                                                  s5_b_re[i], s5_b_im[i], s5_c_re[i], s5_c_im[i])
        o_s5 = _s5(s_u.reshape(batch, seq, S5_WIDTH), a_bar, bre, bim, cre, cim,
                   s5_d[i][None, :], s5_w_glu[i].astype(BF16), s5_b_glu[i][None, :],
                   s5_norm_w[i][None, :], batch, seq, ts)

        lam_init = 0.8 - 0.6 * math.exp(-0.3 * i)
        lam_p = jnp.zeros((SUBLANES, LANES), F32)
        for r, t in enumerate((diff_lambda_q1, diff_lambda_k1, diff_lambda_q2, diff_lambda_k2)):
            lam_p = lam_p.at[r, :DIFF_DIM].set(t[i].astype(F32))
        o_diff = _diff_attn(dqkv, lam_p, diff_norm_w[i][None, :], batch, seq, tq, lam_init)

        x2 = _outproj(x2, o_gdn, o_s5.reshape(batch * seq, S5_WIDTH), o_diff, w_out[i].astype(BF16),
                      batch, seq, tm)

        cw = jnp.zeros((FFN_NCHUNK, SUBLANES, 2 * FFN_CHUNK), F32)
        cw = cw.at[:, :FFN_CONV, :].set(_ffn_chunked(ffn_conv_w[i]))
        x2 = _ffn(x2, ffn_norm_w[i][None, :], _ffn_chunked(ffn_w_up[i]).astype(BF16), cw,
                  _ffn_chunked(ffn_conv_b[i][None, :]), ffn_w_down[i].astype(BF16),
                  final_norm_w[None, :], batch, seq, tm, final_norm=(i == depth - 1))
    return x2.reshape(batch, seq, D_MODEL)
```

```python
import functools
import math

import jax
import jax.numpy as jnp
from jax import lax
from jax.experimental import pallas as pl
from jax.experimental.pallas import tpu as pltpu

F32 = jnp.float32
BF16 = jnp.bfloat16
HIGHEST = lax.Precision.HIGHEST

D_MODEL = 1024
EPS = 1e-6
GDN_HEADS = 4
GDN_DIM = 128
GDN_WIDTH = GDN_HEADS * GDN_DIM
GDN_CONV = 4
GDN_CHUNK = 64
GDN_INV_BLOCK = 16
S5_GROUP = 16
S5_GROUPS = 32
S5_WIDTH = S5_GROUPS * S5_GROUP
S5_STATE = 64
S5_NSTATE = S5_GROUPS * S5_STATE
S5_HALF_IN = S5_WIDTH // 2
S5_HALF_ST = S5_NSTATE // 2
DIFF_HEADS = 4
DIFF_DIM = 64
DIFF_V = 2 * DIFF_DIM
DIFF_WIDTH = DIFF_HEADS * DIFF_V
MIX_WIDTH = GDN_WIDTH + S5_WIDTH + DIFF_WIDTH
D_FF = 2816
FFN_CONV = 3
FFN_CHUNK = 256
FFN_NCHUNK = D_FF // FFN_CHUNK
FFN_AHEAD = 2

LANES = 128
SUBLANES = 8
VMEM_LIMIT = 56 * 1024 * 1024

INF_QKV = 3 * GDN_WIDTH
INF_WIDTH = INF_QKV + GDN_WIDTH + LANES


def _dot(a, b):
    return jnp.dot(a, b, preferred_element_type=F32)


def _dot_nt(a, b, precision=None):
    return lax.dot_general(a, b, (((1,), (1,)), ((), ())), precision=precision,
                           preferred_element_type=F32)


def _dot_tn(a, b):
    return lax.dot_general(a, b, (((0,), (0,)), ((), ())), preferred_element_type=F32)


def _sigmoid(x):
    return 1.0 / (1.0 + jnp.exp(-x))


def _silu(x):
    return x * _sigmoid(x)


def _params(sem):
    return pltpu.CompilerParams(dimension_semantics=sem, vmem_limit_bytes=VMEM_LIMIT)


def _inproj_kernel(x_ref, nw_ref, wf_ref, ws_ref, wa_ref, of_ref, os_ref, oa_ref):
    x = x_ref[...]
    ms = jnp.mean(x * x, axis=-1, keepdims=True)
    h = ((x * lax.rsqrt(ms + EPS)) * nw_ref[...]).astype(BF16)
    of_ref[...] = _dot(h, wf_ref[...])
    os_ref[...] = _dot(h, ws_ref[...])
    oa_ref[...] = _dot(h, wa_ref[...]).astype(BF16)


def _inproj(x2, nw, wf, ws, wa, batch, seq, tm):
    nt = seq // tm
    tokens = batch * seq
    row = lambda b, t: (b * nt + t, 0)
    const = lambda b, t: (0, 0)
    return pl.pallas_call(
        _inproj_kernel,
        grid=(batch, nt),
        in_specs=[pl.BlockSpec((tm, D_MODEL), row),
                  pl.BlockSpec((1, D_MODEL), const),
                  pl.BlockSpec((D_MODEL, INF_WIDTH), const),
                  pl.BlockSpec((D_MODEL, S5_WIDTH), const),
                  pl.BlockSpec((D_MODEL, 3 * DIFF_WIDTH), const)],
        out_specs=[pl.BlockSpec((tm, INF_WIDTH), row),
                   pl.BlockSpec((tm, S5_WIDTH), row),
                   pl.BlockSpec((tm, 3 * DIFF_WIDTH), row)],
        out_shape=[jax.ShapeDtypeStruct((tokens, INF_WIDTH), F32),
                   jax.ShapeDtypeStruct((tokens, S5_WIDTH), F32),
                   jax.ShapeDtypeStruct((tokens, 3 * DIFF_WIDTH), BF16)],
        compiler_params=_params(("parallel", "parallel")),
        name="inproj",
    )(x2, nw, wf, ws, wa)


def _unit_lower_inverse(mats, row, col):
    shift = int(math.log2(GDN_INV_BLOCK))
    same = (row >> shift) == (col >> shift)
    eye = (row == col).astype(F32)
    mm = lambda ps, qs: [_dot(p.astype(BF16), q.astype(BF16)) for p, q in zip(ps, qs)]
    add = lambda ps, qs: [p + q for p, q in zip(ps, qs)]
    ad = [jnp.where(same, a, 0.0) for a in mats]
    ao = [jnp.where(same, 0.0, a) for a in mats]
    p = [eye - a for a in ad]
    x = mm(ad, ad)
    p = add(p, mm(p, x))
    x = mm(x, x)
    p = add(p, mm(p, x))
    x = mm(x, x)
    td = add(p, mm(p, x))
    n1 = mm(td, ao)
    n2 = mm(n1, n1)
    n3 = mm(n1, n2)
    return mm([eye - a + b - d for a, b, d in zip(n1, n2, n3)], td)


def _gdn_kernel(qkv_ref, z_ref, ba_ref, cw_ref, gp_ref, nw_ref, o_ref, xs_ref, q_s, k_s, v_s, st_ref,
                *, tl):
    c = GDN_CHUNK
    ti = pl.program_id(1)

    @pl.when(ti == 0)
    def _():
        xs_ref[0:SUBLANES, :] = jnp.zeros((SUBLANES, INF_QKV), F32)
        st_ref[...] = jnp.zeros_like(st_ref)

    xs_ref[SUBLANES:SUBLANES + tl, :] = qkv_ref[...]
    for h in range(GDN_HEADS):
        for part, dst in enumerate((q_s, k_s, v_s)):
            lo = part * GDN_WIDTH + h * GDN_DIM
            acc = jnp.zeros((tl, GDN_DIM), F32)
            for j in range(GDN_CONV):
                off = SUBLANES - (GDN_CONV - 1) + j
                acc = acc + xs_ref[off:off + tl, lo:lo + GDN_DIM] * cw_ref[j:j + 1, lo:lo + GDN_DIM]
            y = _silu(acc)
            if part < 2:
                y = y * lax.rsqrt(jnp.sum(y * y, axis=-1, keepdims=True) + 1e-6)
            if part == 0:
                y = y * (GDN_DIM ** -0.5)
            dst[h] = y
    xs_ref[0:SUBLANES, :] = xs_ref[tl:tl + SUBLANES, :]

    ba = ba_ref[...]
    beta_all = _sigmoid(ba)
    sp = ba + gp_ref[1:2, :]
    softplus = jnp.maximum(sp, 0.0) + jnp.log(1.0 + jnp.exp(-jnp.abs(sp)))
    g_all = -jnp.exp(gp_ref[0:1, :]) * softplus

    row = lax.broadcasted_iota(jnp.int32, (c, c), 0)
    col = lax.broadcasted_iota(jnp.int32, (c, c), 1)
    incl = row >= col
    strict = row > col
    tri = incl.astype(F32)
    sel_r = lax.broadcasted_iota(jnp.int32, (SUBLANES, LANES), 0)
    sel_c = lax.broadcasted_iota(jnp.int32, (SUBLANES, LANES), 1)
    sel = (sel_c == sel_r + GDN_HEADS).astype(F32)

    nc = tl // c
    nh = GDN_HEADS
    units = [(ci, h) for ci in range(nc) for h in range(nh)]

    def rows(ci):
        return slice(ci * c, (ci + 1) * c)

    gcol = [jnp.dot(tri, g_all[rows(ci), :], precision=HIGHEST, preferred_element_type=F32)
            for ci in range(nc)]
    grow = [_dot_nt(sel, gcol[ci], precision=HIGHEST) for ci in range(nc)]

    q = [q_s[h, rows(ci), :] for ci, h in units]
    k = [k_s[h, rows(ci), :] for ci, h in units]
    v = [v_s[h, rows(ci), :] for ci, h in units]
    beta = [beta_all[rows(ci), h:h + 1] for ci, h in units]
    gc = [gcol[ci][:, nh + h:nh + h + 1] for ci, h in units]
    gr = [grow[ci][h:h + 1, :] for ci, h in units]
    decay = [jnp.where(incl, jnp.exp(jnp.where(incl, a - b, 0.0)), 0.0) for a, b in zip(gc, gr)]
    egc = [jnp.exp(a) for a in gc]
    kb = [a * b for a, b in zip(k, beta)]
    kbf = [a.astype(BF16) for a in k]
    kk = [_dot_nt(a.astype(BF16), b) for a, b in zip(kb, kbf)]
    qk = [_dot_nt(a.astype(BF16), b) for a, b in zip(q, kbf)]
    a_low = [jnp.where(strict, a * d, 0.0) for a, d in zip(kk, decay)]
    attn = [jnp.where(incl, a * d, 0.0).astype(BF16) for a, d in zip(qk, decay)]
    t_inv = [t.astype(BF16) for t in _unit_lower_inverse(a_low, row, col)]
    u_mat = [_dot(t, (a * b).astype(BF16)) for t, a, b in zip(t_inv, v, beta)]
    w_mat = [_dot(t, (a * e).astype(BF16)).astype(BF16) for t, a, e in zip(t_inv, kb, egc)]
    qg = [(a * e).astype(BF16) for a, e in zip(q, egc)]
    g_last = [a[c - 1:c, :] for a in gc]
    kd = [(a * jnp.exp(gl - g)).astype(BF16) for a, gl, g in zip(k, g_last, gc)]
    eg_last = [jnp.exp(gl) for gl in g_last]

    state = [st_ref[h] for h in range(nh)]
    for ci in range(nc):
        us = [ci * nh + h for h in range(nh)]
        sb = [s.astype(BF16) for s in state]
        ws = [_dot(w_mat[u], sb[h]) for h, u in enumerate(us)]
        qs = [_dot(qg[u], sb[h]) for h, u in enumerate(us)]
        v_new = [(u_mat[u] - ws[h]).astype(BF16) for h, u in enumerate(us)]
        o = [qs[h] + _dot(attn[u], v_new[h]) for h, u in enumerate(us)]
        state = [state[h] * eg_last[u] + _dot_tn(kd[u], v_new[h]) for h, u in enumerate(us)]
        for h in range(nh):
            on = o[h] * lax.rsqrt(jnp.mean(o[h] * o[h], axis=-1, keepdims=True) + EPS) * nw_ref[...]
            zz = z_ref[rows(ci), h * GDN_DIM:(h + 1) * GDN_DIM]
            o_ref[rows(ci), h * GDN_DIM:(h + 1) * GDN_DIM] = (on * _silu(zz)).astype(BF16)
    for h in range(nh):
        st_ref[h] = state[h]


def _gdn(inf, conv_w, gate_p, norm_w, batch, seq, tl):
    nt = seq // tl
    tokens = batch * seq
    row = lambda b, t: (b * nt + t, 0)
    const = lambda b, t: (0, 0)
    return pl.pallas_call(
        functools.partial(_gdn_kernel, tl=tl),
        grid=(batch, nt),
        in_specs=[pl.BlockSpec((tl, INF_QKV), row),
                  pl.BlockSpec((tl, GDN_WIDTH), lambda b, t: (b * nt + t, INF_QKV // GDN_WIDTH)),
                  pl.BlockSpec((tl, LANES), lambda b, t: (b * nt + t, (INF_QKV + GDN_WIDTH) // LANES)),
                  pl.BlockSpec((GDN_CONV, INF_QKV), const),
                  pl.BlockSpec((SUBLANES, LANES), const),
                  pl.BlockSpec((1, GDN_DIM), const)],
        out_specs=pl.BlockSpec((tl, GDN_WIDTH), row),
        out_shape=jax.ShapeDtypeStruct((tokens, GDN_WIDTH), BF16),
        scratch_shapes=[pltpu.VMEM((tl + SUBLANES, INF_QKV), F32),
                        pltpu.VMEM((GDN_HEADS, tl, GDN_DIM), F32),
                        pltpu.VMEM((GDN_HEADS, tl, GDN_DIM), F32),
                        pltpu.VMEM((GDN_HEADS, tl, GDN_DIM), F32),
                        pltpu.VMEM((GDN_HEADS, GDN_DIM, GDN_DIM), F32)],
        compiler_params=_params(("parallel", "arbitrary")),
        name="gdn",
    )(inf, inf, inf, conv_w, gate_p, norm_w)


S5_SCAN_LANES = 512


def _gelu_tanh(x):
    return 0.5 * x * (1.0 + jnp.tanh(math.sqrt(2.0 / math.pi) * (x + 0.044715 * (x * x * x))))


def _s5_kernel(u_ref, a_ref, bre_ref, bim_ref, cre_ref, cim_ref, d_ref, wg_ref, bg_ref, nw_ref, o_ref,
               sre_ref, sim_ref, *, batch, tl):
    rows = batch * tl

    @pl.when(pl.program_id(0) == 0)
    def _():
        sre_ref[0:batch, :] = jnp.zeros((batch, S5_NSTATE), F32)
        sim_ref[0:batch, :] = jnp.zeros((batch, S5_NSTATE), F32)

    u_bm = u_ref[...].reshape(rows, S5_WIDTH)
    r_i = lax.broadcasted_iota(jnp.int32, (rows, rows), 0)
    c_i = lax.broadcasted_iota(jnp.int32, (rows, rows), 1)
    to_tm = (c_i == (r_i % batch) * tl + r_i // batch).astype(BF16)
    to_bm = (c_i == (r_i % tl) * batch + r_i // tl).astype(BF16)
    u_hi = u_bm.astype(BF16)
    u_lo = (u_bm - u_hi.astype(F32)).astype(BF16)
    u_hi_tm = _dot(to_tm, u_hi)
    u = u_hi_tm + _dot(to_tm, u_lo)
    ub = u_hi_tm.astype(BF16)
    for hf in range(2):
        cin = slice(hf * S5_HALF_IN, (hf + 1) * S5_HALF_IN)
        cst = slice(hf * S5_HALF_ST, (hf + 1) * S5_HALF_ST)
        sre_ref[batch:batch + rows, cst] = _dot(ub[:, cin], bre_ref[hf])
        sim_ref[batch:batch + rows, cst] = _dot(ub[:, cin], bim_ref[hf])

    def scan(lc):
        cols = slice(lc * S5_SCAN_LANES, (lc + 1) * S5_SCAN_LANES)
        a_re = jnp.broadcast_to(a_ref[0:1, cols], (batch, S5_SCAN_LANES))
        a_im = jnp.broadcast_to(a_ref[1:2, cols], (batch, S5_SCAN_LANES))
        s_re, s_im = sre_ref[0:batch, cols], sim_ref[0:batch, cols]
        for t in range(tl):
            r = slice((t + 1) * batch, (t + 2) * batch)
            s_re, s_im = (a_re * s_re - a_im * s_im + sre_ref[r, cols],
                          a_re * s_im + a_im * s_re + sim_ref[r, cols])
            sre_ref[r, cols] = s_re
            sim_ref[r, cols] = s_im
        sre_ref[0:batch, cols] = s_re
        sim_ref[0:batch, cols] = s_im

    ys = []
    chunks_per_half = S5_HALF_ST // S5_SCAN_LANES
    for hf in range(2):
        for lc in range(hf * chunks_per_half, (hf + 1) * chunks_per_half):
            scan(lc)
        cst = slice(hf * S5_HALF_ST, (hf + 1) * S5_HALF_ST)
        s_re = sre_ref[batch:batch + rows, cst].astype(BF16)
        s_im = sim_ref[batch:batch + rows, cst].astype(BF16)
        ys.append(_dot(s_re, cre_ref[hf]) - _dot(s_im, cim_ref[hf]))
    y = jnp.concatenate(ys, axis=-1) + d_ref[...] * u
    y = _gelu_tanh(y)
    y = y * _sigmoid(_dot(y.astype(BF16), wg_ref[...]) + bg_ref[...])
    y = (y * lax.rsqrt(jnp.mean(y * y, axis=-1, keepdims=True) + EPS) * nw_ref[...]).astype(BF16)
    o_ref[...] = _dot(to_bm, y).astype(BF16).reshape(batch, tl, S5_WIDTH)


def _s5(u3, a_bar, bre, bim, cre, cim, d_skip, w_glu, b_glu, norm_w, batch, seq, tl):
    rows = batch * tl
    blk = lambda t: (0, t, 0)
    c2 = lambda t: (0, 0)
    c3 = lambda t: (0, 0, 0)
    return pl.pallas_call(
        functools.partial(_s5_kernel, batch=batch, tl=tl),
        grid=(seq // tl,),
        in_specs=[pl.BlockSpec((batch, tl, S5_WIDTH), blk),
                  pl.BlockSpec((SUBLANES, S5_NSTATE), c2),
                  pl.BlockSpec((2, S5_HALF_IN, S5_HALF_ST), c3),
                  pl.BlockSpec((2, S5_HALF_IN, S5_HALF_ST), c3),
                  pl.BlockSpec((2, S5_HALF_ST, S5_HALF_IN), c3),
                  pl.BlockSpec((2, S5_HALF_ST, S5_HALF_IN), c3),
                  pl.BlockSpec((1, S5_WIDTH), c2),
                  pl.BlockSpec((S5_WIDTH, S5_WIDTH), c2),
                  pl.BlockSpec((1, S5_WIDTH), c2),
                  pl.BlockSpec((1, S5_WIDTH), c2)],
        out_specs=pl.BlockSpec((batch, tl, S5_WIDTH), blk),
        out_shape=jax.ShapeDtypeStruct((batch, seq, S5_WIDTH), BF16),
        scratch_shapes=[pltpu.VMEM((batch + rows, S5_NSTATE), F32),
                        pltpu.VMEM((batch + rows, S5_NSTATE), F32)],
        compiler_params=_params(("arbitrary",)),
        name="s5",
    )(u3, a_bar, bre, bim, cre, cim, d_skip, w_glu, b_glu, norm_w)


def _s5_operators(lam_re, lam_im, log_dt, b_re, b_im, c_re, c_im):
    l_re = jnp.minimum(lam_re.astype(F32), -1e-4)
    l_im = lam_im.astype(F32)
    dt = jnp.exp(log_dt.astype(F32))[:, None]
    mag = jnp.exp(l_re * dt)
    a_re = mag * jnp.cos(l_im * dt)
    a_im = mag * jnp.sin(l_im * dt)
    den = l_re * l_re + l_im * l_im
    f_re = ((a_re - 1.0) * l_re + a_im * l_im) / den
    f_im = (a_im * l_re - (a_re - 1.0) * l_im) / den
    bb_re = f_re[..., None] * b_re.astype(F32) - f_im[..., None] * b_im.astype(F32)
    bb_im = f_re[..., None] * b_im.astype(F32) + f_im[..., None] * b_re.astype(F32)
    a_bar = jnp.zeros((SUBLANES, S5_NSTATE), F32)
    a_bar = a_bar.at[0].set(a_re.reshape(-1)).at[1].set(a_im.reshape(-1))
    eye = jnp.eye(S5_GROUPS, dtype=F32)

    def b_op(t):
        full = jnp.einsum('gph,gk->ghkp', t, eye).reshape(S5_WIDTH, S5_NSTATE)
        return jnp.stack([full[:S5_HALF_IN, :S5_HALF_ST], full[S5_HALF_IN:, S5_HALF_ST:]]).astype(BF16)

    def c_op(t):
        full = jnp.einsum('ghp,gk->gpkh', t, eye).reshape(S5_NSTATE, S5_WIDTH)
        return jnp.stack([full[:S5_HALF_ST, :S5_HALF_IN], full[S5_HALF_ST:, S5_HALF_IN:]]).astype(BF16)

    return a_bar, b_op(bb_re), b_op(bb_im), c_op(c_re.astype(F32)), c_op(c_im.astype(F32))


ATT_ROWS = 32


def _diff_kernel(q_ref, k_ref, v_ref, lp_ref, nw_ref, o_ref, m_ref, l_ref, acc_ref, *, tq, lam_init):
    qi = pl.program_id(1)
    nh = DIFF_HEADS
    lane = lax.broadcasted_iota(jnp.int32, (tq, DIFF_V), 1)
    scale = DIFF_DIM ** -0.5 * math.log2(math.e)
    qs = []
    for h in range(nh):
        q = q_ref[:, h * DIFF_V:(h + 1) * DIFF_V]
        zero = jnp.zeros_like(q)
        q0 = jnp.where(lane < DIFF_DIM, q, zero)
        q1 = jnp.where(lane < DIFF_DIM, zero, q)
        qs.append((jnp.concatenate([q0, q1], axis=0).astype(F32) * scale).astype(BF16))

    m_ref[...] = jnp.full_like(m_ref, -jnp.inf)
    l_ref[...] = jnp.zeros_like(l_ref)
    acc_ref[...] = jnp.zeros_like(acc_ref)

    def update(kb, masked, nk=1):
        tk = nk * tq
        r = pl.ds(pl.multiple_of(kb * tq, tq), tk)
        hs = range(nh)
        kk = [k_ref[r, h * DIFF_V:(h + 1) * DIFF_V] for h in hs]
        vv = [v_ref[r, h * DIFF_V:(h + 1) * DIFF_V] for h in hs]
        s = {0: _dot_nt(qs[0], kk[0])}
        for h in hs:
            if h + 1 < nh:
                s[h + 1] = _dot_nt(qs[h + 1], kk[h + 1])
            pg, ag = [], []
            for g in range(2 * tq // ATT_ROWS):
                r0 = g * ATT_ROWS
                sg = s[h][r0:r0 + ATT_ROWS, :]
                if masked:
                    rr = lax.broadcasted_iota(jnp.int32, (ATT_ROWS, tq), 0) + (r0 % tq)
                    cc = lax.broadcasted_iota(jnp.int32, (ATT_ROWS, tq), 1)
                    sg = jnp.where(rr >= cc, sg, -jnp.inf)
                tiles = [sg[:, j * LANES:(j + 1) * LANES] for j in range(tk // LANES)]
                m_cur = jnp.max(functools.reduce(jnp.maximum, tiles), axis=-1, keepdims=True)
                m_prev = m_ref[h, r0:r0 + ATT_ROWS, :]
                m_new = jnp.maximum(m_prev, m_cur)
                a = jnp.exp2(m_prev - m_new)
                p = [jnp.exp2(t - m_new) for t in tiles]
                l_ref[h, r0:r0 + ATT_ROWS, :] = a * l_ref[h, r0:r0 + ATT_ROWS, :] + functools.reduce(jnp.add, p)
                m_ref[h, r0:r0 + ATT_ROWS, :] = m_new
                pg.append(jnp.concatenate([t.astype(BF16) for t in p], axis=-1))
                ag.append(a)
            pv = _dot(jnp.concatenate(pg, axis=0), vv[h])
            acc_ref[h] = jnp.concatenate(ag, axis=0) * acc_ref[h] + pv

    def pair(i, carry):
        update(2 * i, False, nk=2)
        return carry

    lax.fori_loop(0, qi // 2, pair, 0)

    @pl.when(qi % 2 == 1)
    def _():
        update(qi - 1, False)

    update(qi, True)

    lp = lp_ref[...]
    lam = (jnp.exp(jnp.sum(lp[0:1, :] * lp[1:2, :], axis=-1, keepdims=True))
           - jnp.exp(jnp.sum(lp[2:3, :] * lp[3:4, :], axis=-1, keepdims=True)) + lam_init)
    for h in range(nh):
        o = acc_ref[h] / jnp.sum(l_ref[h], axis=-1, keepdims=True)
        o = o[:tq, :] - lam * o[tq:, :]
        on = o * lax.rsqrt(jnp.mean(o * o, axis=-1, keepdims=True) + EPS) * nw_ref[...]
        o_ref[:, h * DIFF_V:(h + 1) * DIFF_V] = (on * (1.0 - lam_init)).astype(BF16)


def _diff_attn(qkv, lam_p, norm_w, batch, seq, tq, lam_init):
    nq = seq // tq
    tokens = batch * seq
    return pl.pallas_call(
        functools.partial(_diff_kernel, tq=tq, lam_init=lam_init),
        grid=(batch, nq),
        in_specs=[pl.BlockSpec((tq, DIFF_WIDTH), lambda b, i: (b * nq + i, 0)),
                  pl.BlockSpec((seq, DIFF_WIDTH), lambda b, i: (b, 1)),
                  pl.BlockSpec((seq, DIFF_WIDTH), lambda b, i: (b, 2)),
                  pl.BlockSpec((SUBLANES, LANES), lambda b, i: (0, 0)),
                  pl.BlockSpec((1, DIFF_V), lambda b, i: (0, 0))],
        out_specs=pl.BlockSpec((tq, DIFF_WIDTH), lambda b, i: (b * nq + i, 0)),
        out_shape=jax.ShapeDtypeStruct((tokens, DIFF_WIDTH), BF16),
        scratch_shapes=[pltpu.VMEM((DIFF_HEADS, 2 * tq, LANES), F32),
                        pltpu.VMEM((DIFF_HEADS, 2 * tq, LANES), F32),
                        pltpu.VMEM((DIFF_HEADS, 2 * tq, DIFF_V), F32)],
        compiler_params=_params(("parallel", "parallel")),
        name="diff_attn",
    )(qkv, qkv, qkv, lam_p, norm_w)


def _outproj_kernel(x_ref, og_ref, os_ref, od_ref, w_ref, o_ref):
    acc = _dot(og_ref[...], w_ref[0:GDN_WIDTH, :])
    acc = acc + _dot(os_ref[...], w_ref[GDN_WIDTH:GDN_WIDTH + S5_WIDTH, :])
    acc = acc + _dot(od_ref[...], w_ref[GDN_WIDTH + S5_WIDTH:MIX_WIDTH, :])
    o_ref[...] = x_ref[...] + acc


def _outproj(x2, o_gdn, o_s5, o_diff, w_out, batch, seq, tm):
    nt = seq // tm
    row = lambda b, t: (b * nt + t, 0)
    return pl.pallas_call(
        _outproj_kernel,
        grid=(batch, nt),
        in_specs=[pl.BlockSpec((tm, D_MODEL), row),
                  pl.BlockSpec((tm, GDN_WIDTH), row),
                  pl.BlockSpec((tm, S5_WIDTH), row),
                  pl.BlockSpec((tm, DIFF_WIDTH), row),
                  pl.BlockSpec((MIX_WIDTH, D_MODEL), lambda b, t: (0, 0))],
        out_specs=pl.BlockSpec((tm, D_MODEL), row),
        out_shape=jax.ShapeDtypeStruct(x2.shape, F32),
        compiler_params=_params(("parallel", "parallel")),
        name="outproj",
    )(x2, o_gdn, o_s5, o_diff, w_out)


def _ffn_kernel(x_ref, nw_ref, wu_ref, cw_ref, cb_ref, wd_ref, fw_ref, o_ref, ub_ref, tail_ref, act_ref,
                *, tm, final_norm):
    @pl.when(pl.program_id(1) == 0)
    def _():
        tail_ref[...] = jnp.zeros_like(tail_ref)

    x = x_ref[...]
    ms = jnp.mean(x * x, axis=-1, keepdims=True)
    h = ((x * lax.rsqrt(ms + EPS)) * nw_ref[...]).astype(BF16)
    nslot = FFN_AHEAD + 1
    for j in range(min(FFN_AHEAD, FFN_NCHUNK)):
        ub_ref[j, SUBLANES:SUBLANES + tm, :] = _dot(h, wu_ref[j])
    for j in range(FFN_NCHUNK):
        slot = j % nslot
        if j + FFN_AHEAD < FFN_NCHUNK:
            ub_ref[(j + FFN_AHEAD) % nslot, SUBLANES:SUBLANES + tm, :] = _dot(h, wu_ref[j + FFN_AHEAD])
        ub_ref[slot, 0:SUBLANES, :] = tail_ref[j]
        tail_ref[j] = ub_ref[slot, tm:tm + SUBLANES, :]
        cv = cb_ref[j]
        for t in range(FFN_CONV):
            off = SUBLANES - (FFN_CONV - 1) + t
            cv = cv + ub_ref[slot, off:off + tm, :] * cw_ref[j, t:t + 1, :]
        act = _silu(cv[:, :FFN_CHUNK]) * cv[:, FFN_CHUNK:]
        act_ref[:, j * FFN_CHUNK:(j + 1) * FFN_CHUNK] = act.astype(BF16)
    y = x + _dot(act_ref[...], wd_ref[...])
    if final_norm:
        y = y * lax.rsqrt(jnp.mean(y * y, axis=-1, keepdims=True) + EPS) * fw_ref[...]
    o_ref[...] = y


def _ffn(x2, norm_w, w_up, conv_w, conv_b, w_down, final_w, batch, seq, tm, final_norm):
    nt = seq // tm
    row = lambda b, t: (b * nt + t, 0)
    c2 = lambda b, t: (0, 0)
    c3 = lambda b, t: (0, 0, 0)
    once = pl.Buffered(1)
    return pl.pallas_call(
        functools.partial(_ffn_kernel, tm=tm, final_norm=final_norm),
        grid=(batch, nt),
        in_specs=[pl.BlockSpec((tm, D_MODEL), row),
                  pl.BlockSpec((1, D_MODEL), c2),
                  pl.BlockSpec((FFN_NCHUNK, D_MODEL, 2 * FFN_CHUNK), c3, pipeline_mode=once),
                  pl.BlockSpec((FFN_NCHUNK, SUBLANES, 2 * FFN_CHUNK), c3),
                  pl.BlockSpec((FFN_NCHUNK, 1, 2 * FFN_CHUNK), c3),
                  pl.BlockSpec((D_FF, D_MODEL), c2, pipeline_mode=once),
                  pl.BlockSpec((1, D_MODEL), c2)],
        out_specs=pl.BlockSpec((tm, D_MODEL), row),
        out_shape=jax.ShapeDtypeStruct(x2.shape, F32),
        scratch_shapes=[pltpu.VMEM((FFN_AHEAD + 1, tm + SUBLANES, 2 * FFN_CHUNK), F32),
                        pltpu.VMEM((FFN_NCHUNK, SUBLANES, 2 * FFN_CHUNK), F32),
                        pltpu.VMEM((tm, D_FF), BF16)],
        compiler_params=_params(("parallel", "arbitrary")),
        name="ffn",
    )(x2, norm_w, w_up, conv_w, conv_b, w_down, final_w)


def _ffn_chunked(t):
    lead = t.shape[:-1]
    t = t.reshape(lead + (2, FFN_NCHUNK, FFN_CHUNK))
    t = jnp.moveaxis(t, -2, 0)
    return t.reshape((FFN_NCHUNK,) + lead + (2 * FFN_CHUNK,))


def kernel(x, attn_norm_w, w_in, gdn_conv_w, gdn_a_log, gdn_dt_bias, gdn_norm_w, s5_lambda_re, s5_lambda_im, s5_log_dt, s5_b_re, s5_b_im, s5_c_re, s5_c_im, s5_d, s5_w_glu, s5_b_glu, s5_norm_w, diff_lambda_q1, diff_lambda_k1, diff_lambda_q2, diff_lambda_k2, diff_norm_w, w_out, ffn_norm_w, ffn_w_up, ffn_conv_w, ffn_conv_b, ffn_w_down, final_norm_w):
    batch, seq, _ = x.shape
    depth = w_in.shape[0]
    tm = min(512, seq)
    tl = min(512, seq)
    tq = min(256, seq)
    ts = min(32, seq)
    assert seq % tm == 0 and seq % tl == 0 and seq % tq == 0 and seq % ts == 0
    assert tl % GDN_CHUNK == 0 and batch % SUBLANES == 0

    x2 = x.reshape(batch * seq, D_MODEL).astype(F32)
    o_qkv, o_z, o_b, o_a, o_su = 0, 3 * GDN_WIDTH, 4 * GDN_WIDTH, 4 * GDN_WIDTH + GDN_HEADS, 4 * GDN_WIDTH + 2 * GDN_HEADS
    o_dq = o_su + S5_WIDTH
    for i in range(depth):
        wi = w_in[i]
        pad = jnp.zeros((D_MODEL, LANES - 2 * GDN_HEADS), wi.dtype)
        wf = jnp.concatenate([wi[:, o_qkv:o_b], wi[:, o_b:o_su], pad], axis=1).astype(BF16)
        ws = wi[:, o_su:o_dq].astype(BF16)
        wa = wi[:, o_dq:].astype(BF16)
        inf, s_u, dqkv = _inproj(x2, attn_norm_w[i][None, :], wf, ws, wa, batch, seq, tm)

        gate_p = jnp.zeros((SUBLANES, LANES), F32)
        gate_p = gate_p.at[0, GDN_HEADS:2 * GDN_HEADS].set(gdn_a_log[i].astype(F32))
        gate_p = gate_p.at[1, GDN_HEADS:2 * GDN_HEADS].set(gdn_dt_bias[i].astype(F32))
        o_gdn = _gdn(inf, gdn_conv_w[i], gate_p, gdn_norm_w[i][None, :].astype(F32), batch, seq, tl)

        a_bar, bre, bim, cre, cim = _s5_operators(s5_lambda_re[i], s5_lambda_im[i], s5_log_dt[i],
                                                  s5_b_re[i], s5_b_im[i], s5_c_re[i], s5_c_im[i])
        o_s5 = _s5(s_u.reshape(batch, seq, S5_WIDTH), a_bar, bre, bim, cre, cim,
                   s5_d[i][None, :], s5_w_glu[i].astype(BF16), s5_b_glu[i][None, :],
                   s5_norm_w[i][None, :], batch, seq, ts)

        lam_init = 0.8 - 0.6 * math.exp(-0.3 * i)
        lam_p = jnp.zeros((SUBLANES, LANES), F32)
        for r, t in enumerate((diff_lambda_q1, diff_lambda_k1, diff_lambda_q2, diff_lambda_k2)):
            lam_p = lam_p.at[r, :DIFF_DIM].set(t[i].astype(F32))
        o_diff = _diff_attn(dqkv, lam_p, diff_norm_w[i][None, :], batch, seq, tq, lam_init)

        x2 = _outproj(x2, o_gdn, o_s5.reshape(batch * seq, S5_WIDTH), o_diff, w_out[i].astype(BF16),
                      batch, seq, tm)

        cw = jnp.zeros((FFN_NCHUNK, SUBLANES, 2 * FFN_CHUNK), F32)
        cw = cw.at[:, :FFN_CONV, :].set(_ffn_chunked(ffn_conv_w[i]))
        x2 = _ffn(x2, ffn_norm_w[i][None, :], _ffn_chunked(ffn_w_up[i]).astype(BF16), cw,
                  _ffn_chunked(ffn_conv_b[i][None, :]), ffn_w_down[i].astype(BF16),
                  final_norm_w[None, :], batch, seq, tm, final_norm=(i == depth - 1))
    return x2.reshape(batch, seq, D_MODEL)
```

```python
import functools
import math

import jax
import jax.numpy as jnp
from jax import lax
from jax.experimental import pallas as pl
from jax.experimental.pallas import tpu as pltpu

F32 = jnp.float32
BF16 = jnp.bfloat16
HIGHEST = lax.Precision.HIGHEST

D_MODEL = 1024
EPS = 1e-6
GDN_HEADS = 4
GDN_DIM = 128
GDN_WIDTH = GDN_HEADS * GDN_DIM
GDN_CONV = 4
GDN_CHUNK = 64
GDN_INV_BLOCK = 16
S5_GROUP = 16
S5_GROUPS = 32
S5_WIDTH = S5_GROUPS * S5_GROUP
S5_STATE = 64
S5_NSTATE = S5_GROUPS * S5_STATE
S5_HALF_IN = S5_WIDTH // 2
S5_HALF_ST = S5_NSTATE // 2
DIFF_HEADS = 4
DIFF_DIM = 64
DIFF_V = 2 * DIFF_DIM
DIFF_WIDTH = DIFF_HEADS * DIFF_V
MIX_WIDTH = GDN_WIDTH + S5_WIDTH + DIFF_WIDTH
D_FF = 2816
FFN_CONV = 3
FFN_CHUNK = 256
FFN_NCHUNK = D_FF // FFN_CHUNK
FFN_AHEAD = 2

LANES = 128
SUBLANES = 8
VMEM_LIMIT = 56 * 1024 * 1024

INF_QKV = 3 * GDN_WIDTH
INF_WIDTH = INF_QKV + GDN_WIDTH + LANES


def _dot(a, b):
    return jnp.dot(a, b, preferred_element_type=F32)


def _dot_nt(a, b, precision=None):
    return lax.dot_general(a, b, (((1,), (1,)), ((), ())), precision=precision,
                           preferred_element_type=F32)


def _dot_tn(a, b):
    return lax.dot_general(a, b, (((0,), (0,)), ((), ())), preferred_element_type=F32)


def _sigmoid(x):
    return 0.5 + 0.5 * jnp.tanh(0.5 * x)


def _silu(x):
    h = 0.5 * x
    return h + h * jnp.tanh(h)


def _params(sem):
    return pltpu.CompilerParams(dimension_semantics=sem, vmem_limit_bytes=VMEM_LIMIT)


def _inproj_kernel(x_ref, nw_ref, wf_ref, ws_ref, wa_ref, of_ref, os_ref, oa_ref):
    x = x_ref[...]
    ms = jnp.mean(x * x, axis=-1, keepdims=True)
    h = ((x * lax.rsqrt(ms + EPS)) * nw_ref[...]).astype(BF16)
    of_ref[...] = _dot(h, wf_ref[...])
    os_ref[...] = _dot(h, ws_ref[...])
    oa_ref[...] = _dot(h, wa_ref[...]).astype(BF16)


def _inproj(x2, nw, wf, ws, wa, batch, seq, tm):
    nt = seq // tm
    tokens = batch * seq
    row = lambda b, t: (b * nt + t, 0)
    const = lambda b, t: (0, 0)
    return pl.pallas_call(
        _inproj_kernel,
        grid=(batch, nt),
        in_specs=[pl.BlockSpec((tm, D_MODEL), row),
                  pl.BlockSpec((1, D_MODEL), const),
                  pl.BlockSpec((D_MODEL, INF_WIDTH), const),
                  pl.BlockSpec((D_MODEL, S5_WIDTH), const),
                  pl.BlockSpec((D_MODEL, 3 * DIFF_WIDTH), const)],
        out_specs=[pl.BlockSpec((tm, INF_WIDTH), row),
                   pl.BlockSpec((tm, S5_WIDTH), row),
                   pl.BlockSpec((tm, 3 * DIFF_WIDTH), row)],
        out_shape=[jax.ShapeDtypeStruct((tokens, INF_WIDTH), F32),
                   jax.ShapeDtypeStruct((tokens, S5_WIDTH), F32),
                   jax.ShapeDtypeStruct((tokens, 3 * DIFF_WIDTH), BF16)],
        compiler_params=_params(("parallel", "parallel")),
        name="inproj",
    )(x2, nw, wf, ws, wa)


def _unit_lower_inverse(mats, row, col):
    shift = int(math.log2(GDN_INV_BLOCK))
    same = (row >> shift) == (col >> shift)
    eye = (row == col).astype(F32)
    mm = lambda ps, qs: [_dot(p, q) for p, q in zip(ps, qs)]
    add = lambda ps, qs: [p + q for p, q in zip(ps, qs)]
    b16 = lambda ps: [p.astype(BF16) for p in ps]
    ad = [jnp.where(same, a, 0.0) for a in mats]
    ao = b16([jnp.where(same, 0.0, a) for a in mats])
    p = [eye - a for a in ad]
    x = b16(ad)
    for _ in range(shift - 1):
        x = b16(mm(x, x))
        p = add(p, mm(b16(p), x))
    td = b16(p)
    n1 = mm(td, ao)
    n1b = b16(n1)
    n2 = mm(n1b, n1b)
    n3 = mm(n1b, b16(n2))
    return mm(b16([eye - a + b - d for a, b, d in zip(n1, n2, n3)]), td)


def _gdn_kernel(qkv_ref, z_ref, ba_ref, cw_ref, gp_ref, nw_ref, o_ref, xs_ref, q_s, k_s, v_s, st_ref,
                *, tl):
    c = GDN_CHUNK
    ti = pl.program_id(1)

    @pl.when(ti == 0)
    def _():
        xs_ref[0:SUBLANES, :] = jnp.zeros((SUBLANES, INF_QKV), F32)
        st_ref[...] = jnp.zeros_like(st_ref)

    xs_ref[SUBLANES:SUBLANES + tl, :] = qkv_ref[...]

    def conv_head(h):
        for part, dst in enumerate((q_s, k_s, v_s)):
            lo = part * GDN_WIDTH + h * GDN_DIM
            acc = jnp.zeros((tl, GDN_DIM), F32)
            for j in range(GDN_CONV):
                off = SUBLANES - (GDN_CONV - 1) + j
                acc = acc + xs_ref[off:off + tl, lo:lo + GDN_DIM] * cw_ref[j:j + 1, lo:lo + GDN_DIM]
            y = _silu(acc)
            if part < 2:
                y = y * lax.rsqrt(jnp.sum(y * y, axis=-1, keepdims=True) + 1e-6)
            if part == 0:
                y = y * (GDN_DIM ** -0.5)
            dst[h] = y

    row = lax.broadcasted_iota(jnp.int32, (c, c), 0)
    col = lax.broadcasted_iota(jnp.int32, (c, c), 1)
    incl = row >= col
    strict = row > col
    tri = incl.astype(F32)
    sel_r = lax.broadcasted_iota(jnp.int32, (SUBLANES, LANES), 0)
    sel_c = lax.broadcasted_iota(jnp.int32, (SUBLANES, LANES), 1)
    sel = (sel_c == sel_r).astype(F32)

    nc = tl // c
    nh = GDN_HEADS

    def rows(ci):
        return slice(ci * c, (ci + 1) * c)

    def gates():
        ba = ba_ref[...]
        sp = ba + gp_ref[1:2, :]
        softplus = jnp.maximum(sp, 0.0) + math.log(2.0) * jnp.log2(
            1.0 + jnp.exp2(-math.log2(math.e) * jnp.abs(sp)))
        g_all = -jnp.exp(gp_ref[0:1, :]) * softplus
        beta_all = _sigmoid(ba)
        is_beta = lax.broadcasted_iota(jnp.int32, (c, LANES), 1) < nh
        grow, bgcol = [], []
        for ci in range(nc):
            gc_c = jnp.dot(tri, g_all[rows(ci), :], precision=HIGHEST, preferred_element_type=F32)
            bgcol.append(jnp.where(is_beta, beta_all[rows(ci), :], gc_c))
            grow.append(_dot_nt(sel, gc_c, precision=HIGHEST))
        return grow, bgcol

    def chunk_operators(units):
        q = [q_s[h, rows(ci), :] for ci, h in units]
        k = [k_s[h, rows(ci), :] for ci, h in units]
        v = [v_s[h, rows(ci), :] for ci, h in units]
        beta = [bgcol[ci][:, h:h + 1] for ci, h in units]
        gc = [bgcol[ci][:, nh + h:nh + h + 1] for ci, h in units]
        gr = [grow[ci][nh + h:nh + h + 1, :] for ci, h in units]
        decay = [jnp.where(incl, jnp.exp(jnp.where(incl, a - b, 0.0)), 0.0) for a, b in zip(gc, gr)]
        egc = [jnp.exp(a) for a in gc]
        kb = [a * b for a, b in zip(k, beta)]
        kbf = [a.astype(BF16) for a in k]
        kk = [_dot_nt(a.astype(BF16), b) for a, b in zip(kb, kbf)]
        qk = [_dot_nt(a.astype(BF16), b) for a, b in zip(q, kbf)]
        a_low = [jnp.where(strict, a * d, 0.0) for a, d in zip(kk, decay)]
        attn = [jnp.where(incl, a * d, 0.0).astype(BF16) for a, d in zip(qk, decay)]
        t_inv = [t.astype(BF16) for t in _unit_lower_inverse(a_low, row, col)]
        u_mat = [_dot(t, (a * b).astype(BF16)) for t, a, b in zip(t_inv, v, beta)]
        w_mat = [_dot(t, (a * e).astype(BF16)).astype(BF16) for t, a, e in zip(t_inv, kb, egc)]
        qg = [(a * e).astype(BF16) for a, e in zip(q, egc)]
        g_last = [a[c - 1:c, :] for a in gc]
        kd = [(a * jnp.exp(gl - g)).astype(BF16) for a, gl, g in zip(k, g_last, gc)]
        kd_u = [_dot_tn(a, b.astype(BF16)) for a, b in zip(kd, u_mat)]
        kd_w = [_dot_tn(a, b).astype(BF16) for a, b in zip(kd, w_mat)]
        return dict(u=u_mat, w=w_mat, qg=qg, attn=attn, kd_u=kd_u, kd_w=kd_w,
                    eg=[jnp.exp(gl) for gl in g_last])

    for h in range(nh):
        conv_head(h)
    xs_ref[0:SUBLANES, :] = xs_ref[tl:tl + SUBLANES, :]
    grow, bgcol = gates()
    units = [(ci, h) for ci in range(nc) for h in range(nh)]
    ops = chunk_operators(units)

    state = [st_ref[h] for h in range(nh)]
    s_in = []
    for ci in range(nc):
        s_in.append([s.astype(BF16) for s in state])
        us = [ci * nh + h for h in range(nh)]
        state = [state[h] * ops['eg'][u] + ops['kd_u'][u] - _dot(ops['kd_w'][u], s_in[ci][h])
                 for h, u in enumerate(us)]
    for h in range(nh):
        st_ref[h] = state[h]

    ws = [_dot(w, s_in[ci][h]) for (ci, h), w in zip(units, ops['w'])]
    qs = [_dot(g, s_in[ci][h]) for (ci, h), g in zip(units, ops['qg'])]
    v_new = [(a - b).astype(BF16) for a, b in zip(ops['u'], ws)]
    o = [a + _dot(t, b) for a, t, b in zip(qs, ops['attn'], v_new)]
    for (ci, h), oo in zip(units, o):
        on = oo * lax.rsqrt(jnp.mean(oo * oo, axis=-1, keepdims=True) + EPS) * nw_ref[...]
        zz = z_ref[rows(ci), h * GDN_DIM:(h + 1) * GDN_DIM]
        o_ref[rows(ci), h * GDN_DIM:(h + 1) * GDN_DIM] = (on * _silu(zz)).astype(BF16)


def _gdn(inf, conv_w, gate_p, norm_w, batch, seq, tl):
    nt = seq // tl
    tokens = batch * seq
    row = lambda b, t: (b * nt + t, 0)
    const = lambda b, t: (0, 0)
    return pl.pallas_call(
        functools.partial(_gdn_kernel, tl=tl),
        grid=(batch, nt),
        in_specs=[pl.BlockSpec((tl, INF_QKV), row),
                  pl.BlockSpec((tl, GDN_WIDTH), lambda b, t: (b * nt + t, INF_QKV // GDN_WIDTH)),
                  pl.BlockSpec((tl, LANES), lambda b, t: (b * nt + t, (INF_QKV + GDN_WIDTH) // LANES)),
                  pl.BlockSpec((GDN_CONV, INF_QKV), const),
                  pl.BlockSpec((SUBLANES, LANES), const),
                  pl.BlockSpec((1, GDN_DIM), const)],
        out_specs=pl.BlockSpec((tl, GDN_WIDTH), row),
        out_shape=jax.ShapeDtypeStruct((tokens, GDN_WIDTH), BF16),
        scratch_shapes=[pltpu.VMEM((tl + SUBLANES, INF_QKV), F32),
                        pltpu.VMEM((GDN_HEADS, tl, GDN_DIM), F32),
                        pltpu.VMEM((GDN_HEADS, tl, GDN_DIM), F32),
                        pltpu.VMEM((GDN_HEADS, tl, GDN_DIM), F32),
                        pltpu.VMEM((GDN_HEADS, GDN_DIM, GDN_DIM), F32)],
        compiler_params=_params(("parallel", "arbitrary")),
        name="gdn",
    )(inf, inf, inf, conv_w, gate_p, norm_w)


S5_SCAN_LANES = 512


def _gelu_tanh(x):
    return 0.5 * x * (1.0 + jnp.tanh(math.sqrt(2.0 / math.pi) * (x + 0.044715 * (x * x * x))))


def _s5_kernel(u_ref, a_ref, bre_ref, bim_ref, cre_ref, cim_ref, d_ref, wg_ref, bg_ref, nw_ref, o_ref,
               sre_ref, sim_ref, *, batch, tl):
    rows = batch * tl

    @pl.when(pl.program_id(0) == 0)
    def _():
        sre_ref[0:batch, :] = jnp.zeros((batch, S5_NSTATE), F32)
        sim_ref[0:batch, :] = jnp.zeros((batch, S5_NSTATE), F32)

    u_bm = u_ref[...].reshape(rows, S5_WIDTH)
    r_i = lax.broadcasted_iota(jnp.int32, (rows, rows), 0)
    c_i = lax.broadcasted_iota(jnp.int32, (rows, rows), 1)
    to_tm = (c_i == (r_i % batch) * tl + r_i // batch).astype(BF16)
    to_bm = (c_i == (r_i % tl) * batch + r_i // tl).astype(BF16)
    u_hi = u_bm.astype(BF16)
    u_lo = (u_bm - u_hi.astype(F32)).astype(BF16)
    u_hi_tm = _dot(to_tm, u_hi)
    u = u_hi_tm + _dot(to_tm, u_lo)
    ub = u_hi_tm.astype(BF16)
    for hf in range(2):
        cin = slice(hf * S5_HALF_IN, (hf + 1) * S5_HALF_IN)
        cst = slice(hf * S5_HALF_ST, (hf + 1) * S5_HALF_ST)
        sre_ref[batch:batch + rows, cst] = _dot(ub[:, cin], bre_ref[hf])
        sim_ref[batch:batch + rows, cst] = _dot(ub[:, cin], bim_ref[hf])

    def scan(lc):
        cols = slice(lc * S5_SCAN_LANES, (lc + 1) * S5_SCAN_LANES)
        a_re = jnp.broadcast_to(a_ref[0:1, cols], (batch, S5_SCAN_LANES))
        a_im = jnp.broadcast_to(a_ref[1:2, cols], (batch, S5_SCAN_LANES))
        s_re, s_im = sre_ref[0:batch, cols], sim_ref[0:batch, cols]
        for t in range(tl):
            r = slice((t + 1) * batch, (t + 2) * batch)
            s_re, s_im = (a_re * s_re - a_im * s_im + sre_ref[r, cols],
                          a_re * s_im + a_im * s_re + sim_ref[r, cols])
            sre_ref[r, cols] = s_re
            sim_ref[r, cols] = s_im
        sre_ref[0:batch, cols] = s_re
        sim_ref[0:batch, cols] = s_im

    ys = []
    chunks_per_half = S5_HALF_ST // S5_SCAN_LANES
    for hf in range(2):
        for lc in range(hf * chunks_per_half, (hf + 1) * chunks_per_half):
            scan(lc)
        cst = slice(hf * S5_HALF_ST, (hf + 1) * S5_HALF_ST)
        s_re = sre_ref[batch:batch + rows, cst].astype(BF16)
        s_im = sim_ref[batch:batch + rows, cst].astype(BF16)
        ys.append(_dot(s_re, cre_ref[hf]) - _dot(s_im, cim_ref[hf]))
    y = jnp.concatenate(ys, axis=-1) + d_ref[...] * u
    y = _gelu_tanh(y)
    y = y * _sigmoid(_dot(y.astype(BF16), wg_ref[...]) + bg_ref[...])
    y = (y * lax.rsqrt(jnp.mean(y * y, axis=-1, keepdims=True) + EPS) * nw_ref[...]).astype(BF16)
    o_ref[...] = _dot(to_bm, y).astype(BF16).reshape(batch, tl, S5_WIDTH)


def _s5(u3, a_bar, bre, bim, cre, cim, d_skip, w_glu, b_glu, norm_w, batch, seq, tl):
    rows = batch * tl
    blk = lambda t: (0, t, 0)
    c2 = lambda t: (0, 0)
    c3 = lambda t: (0, 0, 0)
    return pl.pallas_call(
        functools.partial(_s5_kernel, batch=batch, tl=tl),
        grid=(seq // tl,),
        in_specs=[pl.BlockSpec((batch, tl, S5_WIDTH), blk),
                  pl.BlockSpec((SUBLANES, S5_NSTATE), c2),
                  pl.BlockSpec((2, S5_HALF_IN, S5_HALF_ST), c3),
                  pl.BlockSpec((2, S5_HALF_IN, S5_HALF_ST), c3),
                  pl.BlockSpec((2, S5_HALF_ST, S5_HALF_IN), c3),
                  pl.BlockSpec((2, S5_HALF_ST, S5_HALF_IN), c3),
                  pl.BlockSpec((1, S5_WIDTH), c2),
                  pl.BlockSpec((S5_WIDTH, S5_WIDTH), c2),
                  pl.BlockSpec((1, S5_WIDTH), c2),
                  pl.BlockSpec((1, S5_WIDTH), c2)],
        out_specs=pl.BlockSpec((batch, tl, S5_WIDTH), blk),
        out_shape=jax.ShapeDtypeStruct((batch, seq, S5_WIDTH), BF16),
        scratch_shapes=[pltpu.VMEM((batch + rows, S5_NSTATE), F32),
                        pltpu.VMEM((batch + rows, S5_NSTATE), F32)],
        compiler_params=_params(("arbitrary",)),
        name="s5",
    )(u3, a_bar, bre, bim, cre, cim, d_skip, w_glu, b_glu, norm_w)


def _s5_operators(lam_re, lam_im, log_dt, b_re, b_im, c_re, c_im):
    depth = lam_re.shape[0]
    l_re = jnp.minimum(lam_re.astype(F32), -1e-4)
    l_im = lam_im.astype(F32)
    dt = jnp.exp(log_dt.astype(F32))[..., None]
    mag = jnp.exp(l_re * dt)
    a_re = mag * jnp.cos(l_im * dt)
    a_im = mag * jnp.sin(l_im * dt)
    den = l_re * l_re + l_im * l_im
    f_re = ((a_re - 1.0) * l_re + a_im * l_im) / den
    f_im = (a_im * l_re - (a_re - 1.0) * l_im) / den
    bb_re = f_re[..., None] * b_re.astype(F32) - f_im[..., None] * b_im.astype(F32)
    bb_im = f_re[..., None] * b_im.astype(F32) + f_im[..., None] * b_re.astype(F32)
    a_bar = jnp.pad(jnp.stack([a_re.reshape(depth, -1), a_im.reshape(depth, -1)], axis=1),
                    ((0, 0), (0, SUBLANES - 2), (0, 0)))
    eye = jnp.eye(S5_GROUPS, dtype=F32)

    def b_op(t):
        full = jnp.einsum('dgph,gk->dghkp', t, eye).reshape(depth, S5_WIDTH, S5_NSTATE)
        return jnp.stack([full[:, :S5_HALF_IN, :S5_HALF_ST], full[:, S5_HALF_IN:, S5_HALF_ST:]],
                         axis=1).astype(BF16)

    def c_op(t):
        full = jnp.einsum('dghp,gk->dgpkh', t, eye).reshape(depth, S5_NSTATE, S5_WIDTH)
        return jnp.stack([full[:, :S5_HALF_ST, :S5_HALF_IN], full[:, S5_HALF_ST:, S5_HALF_IN:]],
                         axis=1).astype(BF16)

    return a_bar, b_op(bb_re), b_op(bb_im), c_op(c_re.astype(F32)), c_op(c_im.astype(F32))


ATT_ROWS = 32


def _diff_kernel(q_ref, k_ref, v_ref, lp_ref, nw_ref, o_ref, m_ref, l_ref, acc_ref, *, tq, lam_init):
    qi = pl.program_id(1)
    nh = DIFF_HEADS
    lane = lax.broadcasted_iota(jnp.int32, (tq, DIFF_V), 1)
    scale = DIFF_DIM ** -0.5 * math.log2(math.e)
    qs = []
    for h in range(nh):
        q = q_ref[:, h * DIFF_V:(h + 1) * DIFF_V]
        zero = jnp.zeros_like(q)
        q0 = jnp.where(lane < DIFF_DIM, q, zero)
        q1 = jnp.where(lane < DIFF_DIM, zero, q)
        qs.append((jnp.concatenate([q0, q1], axis=0).astype(F32) * scale).astype(BF16))

    m_ref[...] = jnp.full_like(m_ref, -jnp.inf)
    l_ref[...] = jnp.zeros_like(l_ref)
    acc_ref[...] = jnp.zeros_like(acc_ref)

    def update(kb, masked, nk=1):
        tk = nk * tq
        r = pl.ds(pl.multiple_of(kb * tq, tq), tk)
        hs = range(nh)
        kk = [k_ref[r, h * DIFF_V:(h + 1) * DIFF_V] for h in hs]
        vv = [v_ref[r, h * DIFF_V:(h + 1) * DIFF_V] for h in hs]
        s = {0: _dot_nt(qs[0], kk[0])}
        for h in hs:
            if h + 1 < nh:
                s[h + 1] = _dot_nt(qs[h + 1], kk[h + 1])
            pg, ag = [], []
            for g in range(2 * tq // ATT_ROWS):
                r0 = g * ATT_ROWS
                sg = s[h][r0:r0 + ATT_ROWS, :]
                if masked:
                    rr = lax.broadcasted_iota(jnp.int32, (ATT_ROWS, tq), 0) + (r0 % tq)
                    cc = lax.broadcasted_iota(jnp.int32, (ATT_ROWS, tq), 1)
                    sg = jnp.where(rr >= cc, sg, -jnp.inf)
                tiles = [sg[:, j * LANES:(j + 1) * LANES] for j in range(tk // LANES)]
                m_cur = jnp.max(functools.reduce(jnp.maximum, tiles), axis=-1, keepdims=True)
                m_prev = m_ref[h, r0:r0 + ATT_ROWS, :]
                m_new = jnp.maximum(m_prev, m_cur)
                a = jnp.exp2(m_prev - m_new)
                p = [jnp.exp2(t - m_new) for t in tiles]
                l_ref[h, r0:r0 + ATT_ROWS, :] = a * l_ref[h, r0:r0 + ATT_ROWS, :] + functools.reduce(jnp.add, p)
                m_ref[h, r0:r0 + ATT_ROWS, :] = m_new
                pg.append(jnp.concatenate([t.astype(BF16) for t in p], axis=-1))
                ag.append(a)
            pv = _dot(jnp.concatenate(pg, axis=0), vv[h])
            acc_ref[h] = jnp.concatenate(ag, axis=0) * acc_ref[h] + pv

    def pair(i, carry):
        update(2 * i, False, nk=2)
        return carry

    lax.fori_loop(0, qi // 2, pair, 0)

    @pl.when(qi % 2 == 1)
    def _():
        update(qi - 1, False)

    update(qi, True)

    lp = lp_ref[...]
    lam = (jnp.exp(jnp.sum(lp[0:1, :] * lp[1:2, :], axis=-1, keepdims=True))
           - jnp.exp(jnp.sum(lp[2:3, :] * lp[3:4, :], axis=-1, keepdims=True)) + lam_init)
    for h in range(nh):
        o = acc_ref[h] / jnp.sum(l_ref[h], axis=-1, keepdims=True)
        o = o[:tq, :] - lam * o[tq:, :]
        on = o * lax.rsqrt(jnp.mean(o * o, axis=-1, keepdims=True) + EPS) * nw_ref[...]
        o_ref[:, h * DIFF_V:(h + 1) * DIFF_V] = (on * (1.0 - lam_init)).astype(BF16)


def _diff_attn(qkv, lam_p, norm_w, batch, seq, tq, lam_init):
    nq = seq // tq
    tokens = batch * seq
    return pl.pallas_call(
        functools.partial(_diff_kernel, tq=tq, lam_init=lam_init),
        grid=(batch, nq),
        in_specs=[pl.BlockSpec((tq, DIFF_WIDTH), lambda b, i: (b * nq + i, 0)),
                  pl.BlockSpec((seq, DIFF_WIDTH), lambda b, i: (b, 1)),
                  pl.BlockSpec((seq, DIFF_WIDTH), lambda b, i: (b, 2)),
                  pl.BlockSpec((SUBLANES, LANES), lambda b, i: (0, 0)),
                  pl.BlockSpec((1, DIFF_V), lambda b, i: (0, 0))],
        out_specs=pl.BlockSpec((tq, DIFF_WIDTH), lambda b, i: (b * nq + i, 0)),
        out_shape=jax.ShapeDtypeStruct((tokens, DIFF_WIDTH), BF16),
        scratch_shapes=[pltpu.VMEM((DIFF_HEADS, 2 * tq, LANES), F32),
                        pltpu.VMEM((DIFF_HEADS, 2 * tq, LANES), F32),
                        pltpu.VMEM((DIFF_HEADS, 2 * tq, DIFF_V), F32)],
        compiler_params=_params(("parallel", "parallel")),
        name="diff_attn",
    )(qkv, qkv, qkv, lam_p, norm_w)


def _outproj_kernel(x_ref, og_ref, os_ref, od_ref, w_ref, o_ref):
    acc = _dot(og_ref[...], w_ref[0:GDN_WIDTH, :])
    acc = acc + _dot(os_ref[...], w_ref[GDN_WIDTH:GDN_WIDTH + S5_WIDTH, :])
    acc = acc + _dot(od_ref[...], w_ref[GDN_WIDTH + S5_WIDTH:MIX_WIDTH, :])
    o_ref[...] = x_ref[...] + acc


def _outproj(x2, o_gdn, o_s5, o_diff, w_out, batch, seq, tm):
    nt = seq // tm
    row = lambda b, t: (b * nt + t, 0)
    return pl.pallas_call(
        _outproj_kernel,
        grid=(batch, nt),
        in_specs=[pl.BlockSpec((tm, D_MODEL), row),
                  pl.BlockSpec((tm, GDN_WIDTH), row),
                  pl.BlockSpec((tm, S5_WIDTH), row),
                  pl.BlockSpec((tm, DIFF_WIDTH), row),
                  pl.BlockSpec((MIX_WIDTH, D_MODEL), lambda b, t: (0, 0))],
        out_specs=pl.BlockSpec((tm, D_MODEL), row),
        out_shape=jax.ShapeDtypeStruct(x2.shape, F32),
        compiler_params=_params(("parallel", "parallel")),
        name="outproj",
    )(x2, o_gdn, o_s5, o_diff, w_out)


def _ffn_kernel(x_ref, nw_ref, wu_ref, cw_ref, cb_ref, wd_ref, fw_ref, o_ref, ub_ref, tail_ref, act_ref,
                *, tm, final_norm):
    @pl.when(pl.program_id(1) == 0)
    def _():
        tail_ref[...] = jnp.zeros_like(tail_ref)

    x = x_ref[...]
    ms = jnp.mean(x * x, axis=-1, keepdims=True)
    h = ((x * lax.rsqrt(ms + EPS)) * nw_ref[...]).astype(BF16)
    nslot = FFN_AHEAD + 1
    def cols(j):
        return [slice(part * D_FF + j * FFN_CHUNK, part * D_FF + (j + 1) * FFN_CHUNK) for part in range(2)]

    def up(j):
        for part, cs in enumerate(cols(j)):
            ub_ref[j % nslot, SUBLANES:SUBLANES + tm, part * FFN_CHUNK:(part + 1) * FFN_CHUNK] = (
                _dot(h, wu_ref[:, cs]))

    for j in range(min(FFN_AHEAD, FFN_NCHUNK)):
        up(j)
    for j in range(FFN_NCHUNK):
        slot = j % nslot
        if j + FFN_AHEAD < FFN_NCHUNK:
            up(j + FFN_AHEAD)
        ub_ref[slot, 0:SUBLANES, :] = tail_ref[j]
        tail_ref[j] = ub_ref[slot, tm:tm + SUBLANES, :]
        cv = []
        for part, cs in enumerate(cols(j)):
            acc = cb_ref[:, cs]
            for t in range(FFN_CONV):
                off = SUBLANES - (FFN_CONV - 1) + t
                acc = acc + (ub_ref[slot, off:off + tm, part * FFN_CHUNK:(part + 1) * FFN_CHUNK]
                             * cw_ref[t:t + 1, cs])
            cv.append(acc)
        act = _silu(cv[0]) * cv[1]
        act_ref[:, j * FFN_CHUNK:(j + 1) * FFN_CHUNK] = act.astype(BF16)
    y = x + _dot(act_ref[...], wd_ref[...])
    if final_norm:
        y = y * lax.rsqrt(jnp.mean(y * y, axis=-1, keepdims=True) + EPS) * fw_ref[...]
    o_ref[...] = y


def _ffn(x2, norm_w, w_up, conv_w, conv_b, w_down, final_w, batch, seq, tm, final_norm):
    nt = seq // tm
    row = lambda b, t: (b * nt + t, 0)
    c2 = lambda b, t: (0, 0)
    c3 = lambda b, t: (0, 0, 0)
    once = pl.Buffered(1)
    return pl.pallas_call(
        functools.partial(_ffn_kernel, tm=tm, final_norm=final_norm),
        grid=(batch, nt),
        in_specs=[pl.BlockSpec((tm, D_MODEL), row),
                  pl.BlockSpec((1, D_MODEL), c2),
                  pl.BlockSpec((D_MODEL, 2 * D_FF), c2, pipeline_mode=once),
                  pl.BlockSpec((SUBLANES, 2 * D_FF), c2),
                  pl.BlockSpec((1, 2 * D_FF), c2),
                  pl.BlockSpec((D_FF, D_MODEL), c2, pipeline_mode=once),
                  pl.BlockSpec((1, D_MODEL), c2)],
        out_specs=pl.BlockSpec((tm, D_MODEL), row),
        out_shape=jax.ShapeDtypeStruct(x2.shape, F32),
        scratch_shapes=[pltpu.VMEM((FFN_AHEAD + 1, tm + SUBLANES, 2 * FFN_CHUNK), F32),
                        pltpu.VMEM((FFN_NCHUNK, SUBLANES, 2 * FFN_CHUNK), F32),
                        pltpu.VMEM((tm, D_FF), BF16)],
        compiler_params=_params(("parallel", "arbitrary")),
        name="ffn",
    )(x2, norm_w, w_up, conv_w, conv_b, w_down, final_w)


def kernel(x, attn_norm_w, w_in, gdn_conv_w, gdn_a_log, gdn_dt_bias, gdn_norm_w, s5_lambda_re, s5_lambda_im, s5_log_dt, s5_b_re, s5_b_im, s5_c_re, s5_c_im, s5_d, s5_w_glu, s5_b_glu, s5_norm_w, diff_lambda_q1, diff_lambda_k1, diff_lambda_q2, diff_lambda_k2, diff_norm_w, w_out, ffn_norm_w, ffn_w_up, ffn_conv_w, ffn_conv_b, ffn_w_down, final_norm_w):
    batch, seq, _ = x.shape
    depth = w_in.shape[0]
    tm = min(512, seq)
    tl = min(512, seq)
    tq = min(256, seq)
    ts = min(32, seq)
    assert seq % tm == 0 and seq % tl == 0 and seq % tq == 0 and seq % ts == 0
    assert tl % GDN_CHUNK == 0 and batch % SUBLANES == 0

    x2 = x.reshape(batch * seq, D_MODEL).astype(F32)
    o_b, o_su = 4 * GDN_WIDTH, 4 * GDN_WIDTH + 2 * GDN_HEADS
    o_dq = o_su + S5_WIDTH
    f32 = lambda t: t.astype(F32)
    w_gdn = jnp.pad(w_in[:, :, :o_su], ((0, 0), (0, 0), (0, LANES - 2 * GDN_HEADS))).astype(BF16)
    w_s5 = w_in[:, :, o_su:o_dq].astype(BF16)
    w_att = w_in[:, :, o_dq:].astype(BF16)
    gate_p = jnp.pad(jnp.stack([f32(gdn_a_log), f32(gdn_dt_bias)], axis=1),
                     ((0, 0), (0, SUBLANES - 2), (GDN_HEADS, LANES - 2 * GDN_HEADS)))
    a_bar, bre, bim, cre, cim = _s5_operators(s5_lambda_re, s5_lambda_im, s5_log_dt,
                                              s5_b_re, s5_b_im, s5_c_re, s5_c_im)
    w_glu = s5_w_glu.astype(BF16)
    lam_p = jnp.pad(jnp.stack([f32(diff_lambda_q1), f32(diff_lambda_k1), f32(diff_lambda_q2),
                               f32(diff_lambda_k2)], axis=1),
                    ((0, 0), (0, SUBLANES - 4), (0, LANES - DIFF_DIM)))
    w_o = w_out.astype(BF16)
    w_up = ffn_w_up.astype(BF16)
    w_down = ffn_w_down.astype(BF16)
    conv_w = jnp.pad(f32(ffn_conv_w), ((0, 0), (0, SUBLANES - FFN_CONV), (0, 0)))
    for i in range(depth):
        inf, s_u, dqkv = _inproj(x2, attn_norm_w[i][None, :], w_gdn[i], w_s5[i], w_att[i], batch, seq, tm)
        o_gdn = _gdn(inf, gdn_conv_w[i], gate_p[i], f32(gdn_norm_w[i][None, :]), batch, seq, tl)
        o_s5 = _s5(s_u.reshape(batch, seq, S5_WIDTH), a_bar[i], bre[i], bim[i], cre[i], cim[i],
                   s5_d[i][None, :], w_glu[i], s5_b_glu[i][None, :], s5_norm_w[i][None, :], batch, seq, ts)
        lam_init = 0.8 - 0.6 * math.exp(-0.3 * i)
        o_diff = _diff_attn(dqkv, lam_p[i], diff_norm_w[i][None, :], batch, seq, tq, lam_init)
        x2 = _outproj(x2, o_gdn, o_s5.reshape(batch * seq, S5_WIDTH), o_diff, w_o[i], batch, seq, tm)
        x2 = _ffn(x2, ffn_norm_w[i][None, :], w_up[i], conv_w[i], ffn_conv_b[i][None, :], w_down[i],
                  final_norm_w[None, :], batch, seq, tm, final_norm=(i == depth - 1))
    return x2.reshape(batch, seq, D_MODEL)
```

```python
import functools
import math

import jax
import jax.numpy as jnp
from jax import lax
from jax.experimental import pallas as pl
from jax.experimental.pallas import tpu as pltpu

F32 = jnp.float32
BF16 = jnp.bfloat16
HIGHEST = lax.Precision.HIGHEST

D_MODEL = 1024
EPS = 1e-6
GDN_HEADS = 4
GDN_DIM = 128
GDN_WIDTH = GDN_HEADS * GDN_DIM
GDN_CONV = 4
GDN_CHUNK = 64
GDN_INV_BLOCK = 16
S5_GROUP = 16
S5_GROUPS = 32
S5_WIDTH = S5_GROUPS * S5_GROUP
S5_STATE = 64
S5_NSTATE = S5_GROUPS * S5_STATE
S5_HALF_IN = S5_WIDTH // 2
S5_HALF_ST = S5_NSTATE // 2
DIFF_HEADS = 4
DIFF_DIM = 64
DIFF_V = 2 * DIFF_DIM
DIFF_WIDTH = DIFF_HEADS * DIFF_V
MIX_WIDTH = GDN_WIDTH + S5_WIDTH + DIFF_WIDTH
D_FF = 2816
FFN_CONV = 3
FFN_CHUNK = 256
FFN_NCHUNK = D_FF // FFN_CHUNK
FFN_AHEAD = 2

LANES = 128
SUBLANES = 8
VMEM_LIMIT = 56 * 1024 * 1024

INF_QKV = 3 * GDN_WIDTH
INF_WIDTH = INF_QKV + GDN_WIDTH + LANES


def _dot(a, b):
    return jnp.dot(a, b, preferred_element_type=F32)


def _dot_nt(a, b, precision=None):
    return lax.dot_general(a, b, (((1,), (1,)), ((), ())), precision=precision,
                           preferred_element_type=F32)


def _dot_tn(a, b):
    return lax.dot_general(a, b, (((0,), (0,)), ((), ())), preferred_element_type=F32)


def _sigmoid(x):
    return 0.5 + 0.5 * jnp.tanh(0.5 * x)


def _silu(x):
    h = 0.5 * x
    return h + h * jnp.tanh(h)


def _params(sem):
    return pltpu.CompilerParams(dimension_semantics=sem, vmem_limit_bytes=VMEM_LIMIT)


def _inproj_kernel(x_ref, nw_ref, wf_ref, ws_ref, wa_ref, of_ref, os_ref, oa_ref):
    x = x_ref[...]
    ms = jnp.mean(x * x, axis=-1, keepdims=True)
    h = ((x * lax.rsqrt(ms + EPS)) * nw_ref[...]).astype(BF16)
    of_ref[...] = _dot(h, wf_ref[...])
    os_ref[...] = _dot(h, ws_ref[...])
    oa_ref[...] = _dot(h, wa_ref[...]).astype(BF16)


def _inproj(x2, nw, wf, ws, wa, batch, seq, tm):
    nt = seq // tm
    tokens = batch * seq
    row = lambda b, t: (b * nt + t, 0)
    const = lambda b, t: (0, 0)
    return pl.pallas_call(
        _inproj_kernel,
        grid=(batch, nt),
        in_specs=[pl.BlockSpec((tm, D_MODEL), row),
                  pl.BlockSpec((1, D_MODEL), const),
                  pl.BlockSpec((D_MODEL, INF_WIDTH), const),
                  pl.BlockSpec((D_MODEL, S5_WIDTH), const),
                  pl.BlockSpec((D_MODEL, 3 * DIFF_WIDTH), const)],
        out_specs=[pl.BlockSpec((tm, INF_WIDTH), row),
                   pl.BlockSpec((tm, S5_WIDTH), row),
                   pl.BlockSpec((tm, 3 * DIFF_WIDTH), row)],
        out_shape=[jax.ShapeDtypeStruct((tokens, INF_WIDTH), F32),
                   jax.ShapeDtypeStruct((tokens, S5_WIDTH), F32),
                   jax.ShapeDtypeStruct((tokens, 3 * DIFF_WIDTH), BF16)],
        compiler_params=_params(("parallel", "parallel")),
        name="inproj",
    )(x2, nw, wf, ws, wa)


def _unit_lower_inverse(mats, row, col):
    shift = int(math.log2(GDN_INV_BLOCK))
    same = (row >> shift) == (col >> shift)
    eye = (row == col).astype(F32)
    mm = lambda ps, qs: [_dot(p, q) for p, q in zip(ps, qs)]
    add = lambda ps, qs: [p + q for p, q in zip(ps, qs)]
    b16 = lambda ps: [p.astype(BF16) for p in ps]
    ad = [jnp.where(same, a, 0.0) for a in mats]
    ao = b16([jnp.where(same, 0.0, a) for a in mats])
    p = [eye - a for a in ad]
    x = b16(ad)
    for _ in range(shift - 1):
        x = b16(mm(x, x))
        p = add(p, mm(b16(p), x))
    td = b16(p)
    n1 = mm(td, ao)
    n1b = b16(n1)
    n2 = mm(n1b, n1b)
    n3 = mm(n1b, b16(n2))
    return mm(b16([eye - a + b - d for a, b, d in zip(n1, n2, n3)]), td)


def _gdn_kernel(qkv_ref, z_ref, ba_ref, cw_ref, gp_ref, nw_ref, o_ref, xs_ref, q_s, k_s, v_s, st_ref,
                *, tl):
    c = GDN_CHUNK
    ti = pl.program_id(1)

    @pl.when(ti == 0)
    def _():
        xs_ref[0:SUBLANES, :] = jnp.zeros((SUBLANES, INF_QKV), F32)
        st_ref[...] = jnp.zeros_like(st_ref)

    xs_ref[SUBLANES:SUBLANES + tl, :] = qkv_ref[...]

    def conv_head(h):
        for part, dst in enumerate((q_s, k_s, v_s)):
            lo = part * GDN_WIDTH + h * GDN_DIM
            acc = jnp.zeros((tl, GDN_DIM), F32)
            for j in range(GDN_CONV):
                off = SUBLANES - (GDN_CONV - 1) + j
                acc = acc + xs_ref[off:off + tl, lo:lo + GDN_DIM] * cw_ref[j:j + 1, lo:lo + GDN_DIM]
            y = _silu(acc)
            if part < 2:
                y = y * lax.rsqrt(jnp.sum(y * y, axis=-1, keepdims=True) + 1e-6)
            if part == 0:
                y = y * (GDN_DIM ** -0.5)
            dst[h] = y

    row = lax.broadcasted_iota(jnp.int32, (c, c), 0)
    col = lax.broadcasted_iota(jnp.int32, (c, c), 1)
    incl = row >= col
    strict = row > col
    tri = incl.astype(F32)
    sel_r = lax.broadcasted_iota(jnp.int32, (SUBLANES, LANES), 0)
    sel_c = lax.broadcasted_iota(jnp.int32, (SUBLANES, LANES), 1)
    sel = (sel_c == sel_r).astype(F32)

    nc = tl // c
    nh = GDN_HEADS

    def rows(ci):
        return slice(ci * c, (ci + 1) * c)

    def gates():
        ba = ba_ref[...]
        sp = ba + gp_ref[1:2, :]
        softplus = jnp.maximum(sp, 0.0) + math.log(2.0) * jnp.log2(
            1.0 + jnp.exp2(-math.log2(math.e) * jnp.abs(sp)))
        g_all = -jnp.exp(gp_ref[0:1, :]) * softplus
        beta_all = _sigmoid(ba)
        is_beta = lax.broadcasted_iota(jnp.int32, (c, LANES), 1) < nh
        grow, bgcol = [], []
        for ci in range(nc):
            gc_c = jnp.dot(tri, g_all[rows(ci), :], precision=HIGHEST, preferred_element_type=F32)
            bgcol.append(jnp.where(is_beta, beta_all[rows(ci), :], gc_c))
            grow.append(_dot_nt(sel, gc_c, precision=HIGHEST))
        return grow, bgcol

    def chunk_operators(units):
        q = [q_s[h, rows(ci), :] for ci, h in units]
        k = [k_s[h, rows(ci), :] for ci, h in units]
        v = [v_s[h, rows(ci), :] for ci, h in units]
        beta = [bgcol[ci][:, h:h + 1] for ci, h in units]
        gc = [bgcol[ci][:, nh + h:nh + h + 1] for ci, h in units]
        gr = [grow[ci][nh + h:nh + h + 1, :] for ci, h in units]
        decay = [jnp.where(incl, jnp.exp(jnp.where(incl, a - b, 0.0)), 0.0) for a, b in zip(gc, gr)]
        egc = [jnp.exp(a) for a in gc]
        kb = [a * b for a, b in zip(k, beta)]
        kbf = [a.astype(BF16) for a in k]
        kk = [_dot_nt(a.astype(BF16), b) for a, b in zip(kb, kbf)]
        qk = [_dot_nt(a.astype(BF16), b) for a, b in zip(q, kbf)]
        a_low = [jnp.where(strict, a * d, 0.0) for a, d in zip(kk, decay)]
        attn = [jnp.where(incl, a * d, 0.0).astype(BF16) for a, d in zip(qk, decay)]
        t_inv = [t.astype(BF16) for t in _unit_lower_inverse(a_low, row, col)]
        u_mat = [_dot(t, (a * b).astype(BF16)) for t, a, b in zip(t_inv, v, beta)]
        w_mat = [_dot(t, (a * e).astype(BF16)).astype(BF16) for t, a, e in zip(t_inv, kb, egc)]
        qg = [(a * e).astype(BF16) for a, e in zip(q, egc)]
        g_last = [a[c - 1:c, :] for a in gc]
        kd = [(a * jnp.exp(gl - g)).astype(BF16) for a, gl, g in zip(k, g_last, gc)]
        kd_u = [_dot_tn(a, b.astype(BF16)) for a, b in zip(kd, u_mat)]
        kd_w = [_dot_tn(a, b).astype(BF16) for a, b in zip(kd, w_mat)]
        return dict(u=u_mat, w=w_mat, qg=qg, attn=attn, kd_u=kd_u, kd_w=kd_w,
                    eg=[jnp.exp(gl) for gl in g_last])

    for h in range(nh):
        conv_head(h)
    xs_ref[0:SUBLANES, :] = xs_ref[tl:tl + SUBLANES, :]
    grow, bgcol = gates()
    units = [(ci, h) for ci in range(nc) for h in range(nh)]
    ops = chunk_operators(units)

    state = [st_ref[h] for h in range(nh)]
    s_in = []
    for ci in range(nc):
        s_in.append([s.astype(BF16) for s in state])
        us = [ci * nh + h for h in range(nh)]
        state = [state[h] * ops['eg'][u] + ops['kd_u'][u] - _dot(ops['kd_w'][u], s_in[ci][h])
                 for h, u in enumerate(us)]
    for h in range(nh):
        st_ref[h] = state[h]

    ws = [_dot(w, s_in[ci][h]) for (ci, h), w in zip(units, ops['w'])]
    qs = [_dot(g, s_in[ci][h]) for (ci, h), g in zip(units, ops['qg'])]
    v_new = [(a - b).astype(BF16) for a, b in zip(ops['u'], ws)]
    o = [a + _dot(t, b) for a, t, b in zip(qs, ops['attn'], v_new)]
    for (ci, h), oo in zip(units, o):
        on = oo * lax.rsqrt(jnp.mean(oo * oo, axis=-1, keepdims=True) + EPS) * nw_ref[...]
        zz = z_ref[rows(ci), h * GDN_DIM:(h + 1) * GDN_DIM]
        o_ref[rows(ci), h * GDN_DIM:(h + 1) * GDN_DIM] = (on * _silu(zz)).astype(BF16)


def _gdn(inf, conv_w, gate_p, norm_w, batch, seq, tl):
    nt = seq // tl
    tokens = batch * seq
    row = lambda b, t: (b * nt + t, 0)
    const = lambda b, t: (0, 0)
    return pl.pallas_call(
        functools.partial(_gdn_kernel, tl=tl),
        grid=(batch, nt),
        in_specs=[pl.BlockSpec((tl, INF_QKV), row),
                  pl.BlockSpec((tl, GDN_WIDTH), lambda b, t: (b * nt + t, INF_QKV // GDN_WIDTH)),
                  pl.BlockSpec((tl, LANES), lambda b, t: (b * nt + t, (INF_QKV + GDN_WIDTH) // LANES)),
                  pl.BlockSpec((GDN_CONV, INF_QKV), const),
                  pl.BlockSpec((SUBLANES, LANES), const),
                  pl.BlockSpec((1, GDN_DIM), const)],
        out_specs=pl.BlockSpec((tl, GDN_WIDTH), row),
        out_shape=jax.ShapeDtypeStruct((tokens, GDN_WIDTH), BF16),
        scratch_shapes=[pltpu.VMEM((tl + SUBLANES, INF_QKV), F32),
                        pltpu.VMEM((GDN_HEADS, tl, GDN_DIM), F32),
                        pltpu.VMEM((GDN_HEADS, tl, GDN_DIM), F32),
                        pltpu.VMEM((GDN_HEADS, tl, GDN_DIM), F32),
                        pltpu.VMEM((GDN_HEADS, GDN_DIM, GDN_DIM), F32)],
        compiler_params=_params(("parallel", "arbitrary")),
        name="gdn",
    )(inf, inf, inf, conv_w, gate_p, norm_w)


S5_SCAN_LANES = 512


def _gelu_tanh(x):
    return 0.5 * x * (1.0 + jnp.tanh(math.sqrt(2.0 / math.pi) * (x + 0.044715 * (x * x * x))))


def _s5_kernel(u_ref, a_ref, bre_ref, bim_ref, cre_ref, cim_ref, d_ref, wg_ref, bg_ref, nw_ref, o_ref,
               sre_ref, sim_ref, *, batch, tl):
    rows = batch * tl

    @pl.when(pl.program_id(0) == 0)
    def _():
        sre_ref[0:batch, :] = jnp.zeros((batch, S5_NSTATE), F32)
        sim_ref[0:batch, :] = jnp.zeros((batch, S5_NSTATE), F32)

    u_bm = u_ref[...].reshape(rows, S5_WIDTH)
    r_i = lax.broadcasted_iota(jnp.int32, (rows, rows), 0)
    c_i = lax.broadcasted_iota(jnp.int32, (rows, rows), 1)
    to_tm = (c_i == (r_i % batch) * tl + r_i // batch).astype(BF16)
    to_bm = (c_i == (r_i % tl) * batch + r_i // tl).astype(BF16)
    u_hi = u_bm.astype(BF16)
    u_lo = (u_bm - u_hi.astype(F32)).astype(BF16)
    u_hi_tm = _dot(to_tm, u_hi)
    u = u_hi_tm + _dot(to_tm, u_lo)
    ub = u_hi_tm.astype(BF16)
    for hf in range(2):
        cin = slice(hf * S5_HALF_IN, (hf + 1) * S5_HALF_IN)
        cst = slice(hf * S5_HALF_ST, (hf + 1) * S5_HALF_ST)
        sre_ref[batch:batch + rows, cst] = _dot(ub[:, cin], bre_ref[hf])
        sim_ref[batch:batch + rows, cst] = _dot(ub[:, cin], bim_ref[hf])

    def scan(lc):
        cols = slice(lc * S5_SCAN_LANES, (lc + 1) * S5_SCAN_LANES)
        a_re = jnp.broadcast_to(a_ref[0:1, cols], (batch, S5_SCAN_LANES))
        a_im = jnp.broadcast_to(a_ref[1:2, cols], (batch, S5_SCAN_LANES))
        s_re, s_im = sre_ref[0:batch, cols], sim_ref[0:batch, cols]
        for t in range(tl):
            r = slice((t + 1) * batch, (t + 2) * batch)
            s_re, s_im = (a_re * s_re - a_im * s_im + sre_ref[r, cols],
                          a_re * s_im + a_im * s_re + sim_ref[r, cols])
            sre_ref[r, cols] = s_re
            sim_ref[r, cols] = s_im
        sre_ref[0:batch, cols] = s_re
        sim_ref[0:batch, cols] = s_im

    ys = []
    chunks_per_half = S5_HALF_ST // S5_SCAN_LANES
    for hf in range(2):
        for lc in range(hf * chunks_per_half, (hf + 1) * chunks_per_half):
            scan(lc)
        cst = slice(hf * S5_HALF_ST, (hf + 1) * S5_HALF_ST)
        s_re = sre_ref[batch:batch + rows, cst].astype(BF16)
        s_im = sim_ref[batch:batch + rows, cst].astype(BF16)
        ys.append(_dot(s_re, cre_ref[hf]) - _dot(s_im, cim_ref[hf]))
    y = jnp.concatenate(ys, axis=-1) + d_ref[...] * u
    y = _gelu_tanh(y)
    y = y * _sigmoid(_dot(y.astype(BF16), wg_ref[...]) + bg_ref[...])
    y = (y * lax.rsqrt(jnp.mean(y * y, axis=-1, keepdims=True) + EPS) * nw_ref[...]).astype(BF16)
    o_ref[...] = _dot(to_bm, y).astype(BF16).reshape(batch, tl, S5_WIDTH)


def _s5(u3, a_bar, bre, bim, cre, cim, d_skip, w_glu, b_glu, norm_w, batch, seq, tl):
    rows = batch * tl
    blk = lambda t: (0, t, 0)
    c2 = lambda t: (0, 0)
    c3 = lambda t: (0, 0, 0)
    return pl.pallas_call(
        functools.partial(_s5_kernel, batch=batch, tl=tl),
        grid=(seq // tl,),
        in_specs=[pl.BlockSpec((batch, tl, S5_WIDTH), blk),
                  pl.BlockSpec((SUBLANES, S5_NSTATE), c2),
                  pl.BlockSpec((2, S5_HALF_IN, S5_HALF_ST), c3),
                  pl.BlockSpec((2, S5_HALF_IN, S5_HALF_ST), c3),
                  pl.BlockSpec((2, S5_HALF_ST, S5_HALF_IN), c3),
                  pl.BlockSpec((2, S5_HALF_ST, S5_HALF_IN), c3),
                  pl.BlockSpec((1, S5_WIDTH), c2),
                  pl.BlockSpec((S5_WIDTH, S5_WIDTH), c2),
                  pl.BlockSpec((1, S5_WIDTH), c2),
                  pl.BlockSpec((1, S5_WIDTH), c2)],
        out_specs=pl.BlockSpec((batch, tl, S5_WIDTH), blk),
        out_shape=jax.ShapeDtypeStruct((batch, seq, S5_WIDTH), BF16),
        scratch_shapes=[pltpu.VMEM((batch + rows, S5_NSTATE), F32),
                        pltpu.VMEM((batch + rows, S5_NSTATE), F32)],
        compiler_params=_params(("arbitrary",)),
        name="s5",
    )(u3, a_bar, bre, bim, cre, cim, d_skip, w_glu, b_glu, norm_w)


def _s5_operators(lam_re, lam_im, log_dt, b_re, b_im, c_re, c_im):
    depth = lam_re.shape[0]
    l_re = jnp.minimum(lam_re.astype(F32), -1e-4)
    l_im = lam_im.astype(F32)
    dt = jnp.exp(log_dt.astype(F32))[..., None]
    mag = jnp.exp(l_re * dt)
    a_re = mag * jnp.cos(l_im * dt)
    a_im = mag * jnp.sin(l_im * dt)
    den = l_re * l_re + l_im * l_im
    f_re = ((a_re - 1.0) * l_re + a_im * l_im) / den
    f_im = (a_im * l_re - (a_re - 1.0) * l_im) / den
    bb_re = f_re[..., None] * b_re.astype(F32) - f_im[..., None] * b_im.astype(F32)
    bb_im = f_re[..., None] * b_im.astype(F32) + f_im[..., None] * b_re.astype(F32)
    a_bar = jnp.pad(jnp.stack([a_re.reshape(depth, -1), a_im.reshape(depth, -1)], axis=1),
                    ((0, 0), (0, SUBLANES - 2), (0, 0)))
    eye = jnp.eye(S5_GROUPS, dtype=F32)

    def b_op(t):
        full = jnp.einsum('dgph,gk->dghkp', t, eye).reshape(depth, S5_WIDTH, S5_NSTATE)
        return jnp.stack([full[:, :S5_HALF_IN, :S5_HALF_ST], full[:, S5_HALF_IN:, S5_HALF_ST:]],
                         axis=1).astype(BF16)

    def c_op(t):
        full = jnp.einsum('dghp,gk->dgpkh', t, eye).reshape(depth, S5_NSTATE, S5_WIDTH)
        return jnp.stack([full[:, :S5_HALF_ST, :S5_HALF_IN], full[:, S5_HALF_ST:, S5_HALF_IN:]],
                         axis=1).astype(BF16)

    return a_bar, b_op(bb_re), b_op(bb_im), c_op(c_re.astype(F32)), c_op(c_im.astype(F32))


ATT_ROWS = 32


def _diff_kernel(q_ref, k_ref, v_ref, lp_ref, nw_ref, o_ref, m_ref, l_ref, acc_ref, *, tq, lam_init):
    qi = pl.program_id(1)
    nh = DIFF_HEADS
    lane = lax.broadcasted_iota(jnp.int32, (tq, DIFF_V), 1)
    scale = DIFF_DIM ** -0.5 * math.log2(math.e)
    qs = []
    for h in range(nh):
        q = q_ref[:, h * DIFF_V:(h + 1) * DIFF_V]
        zero = jnp.zeros_like(q)
        q0 = jnp.where(lane < DIFF_DIM, q, zero)
        q1 = jnp.where(lane < DIFF_DIM, zero, q)
        qs.append((jnp.concatenate([q0, q1], axis=0).astype(F32) * scale).astype(BF16))

    m_ref[...] = jnp.full_like(m_ref, -jnp.inf)
    l_ref[...] = jnp.zeros_like(l_ref)
    acc_ref[...] = jnp.zeros_like(acc_ref)

    def update(kb, masked, nk=1):
        tk = nk * tq
        r = pl.ds(pl.multiple_of(kb * tq, tq), tk)
        hs = range(nh)
        kk = [k_ref[r, h * DIFF_V:(h + 1) * DIFF_V] for h in hs]
        vv = [v_ref[r, h * DIFF_V:(h + 1) * DIFF_V] for h in hs]
        s = {0: _dot_nt(qs[0], kk[0])}
        for h in hs:
            if h + 1 < nh:
                s[h + 1] = _dot_nt(qs[h + 1], kk[h + 1])
            pg, ag = [], []
            for g in range(2 * tq // ATT_ROWS):
                r0 = g * ATT_ROWS
                sg = s[h][r0:r0 + ATT_ROWS, :]
                if masked:
                    rr = lax.broadcasted_iota(jnp.int32, (ATT_ROWS, tk), 0) + (r0 % tq)
                    cc = lax.broadcasted_iota(jnp.int32, (ATT_ROWS, tk), 1) - (nk - 1) * tq
                    sg = jnp.where(rr >= cc, sg, -jnp.inf)
                tiles = [sg[:, j * LANES:(j + 1) * LANES] for j in range(tk // LANES)]
                m_cur = jnp.max(functools.reduce(jnp.maximum, tiles), axis=-1, keepdims=True)
                m_prev = m_ref[h, r0:r0 + ATT_ROWS, :]
                m_new = jnp.maximum(m_prev, m_cur)
                a = jnp.exp2(m_prev - m_new)
                p = [jnp.exp2(t - m_new) for t in tiles]
                l_ref[h, r0:r0 + ATT_ROWS, :] = a * l_ref[h, r0:r0 + ATT_ROWS, :] + functools.reduce(jnp.add, p)
                m_ref[h, r0:r0 + ATT_ROWS, :] = m_new
                pg.append(jnp.concatenate([t.astype(BF16) for t in p], axis=-1))
                ag.append(a)
            pv = _dot(jnp.concatenate(pg, axis=0), vv[h])
            acc_ref[h] = jnp.concatenate(ag, axis=0) * acc_ref[h] + pv

    def pair(i, carry):
        update(2 * i, False, nk=2)
        return carry

    lax.fori_loop(0, qi // 2, pair, 0)

    @pl.when(qi % 2 == 1)
    def _():
        update(qi - 1, True, nk=2)

    @pl.when(qi % 2 == 0)
    def _():
        update(qi, True)

    lp = lp_ref[...]
    lam = (jnp.exp(jnp.sum(lp[0:1, :] * lp[1:2, :], axis=-1, keepdims=True))
           - jnp.exp(jnp.sum(lp[2:3, :] * lp[3:4, :], axis=-1, keepdims=True)) + lam_init)
    for h in range(nh):
        o = acc_ref[h] / jnp.sum(l_ref[h], axis=-1, keepdims=True)
        o = o[:tq, :] - lam * o[tq:, :]
        on = o * lax.rsqrt(jnp.mean(o * o, axis=-1, keepdims=True) + EPS) * nw_ref[...]
        o_ref[:, h * DIFF_V:(h + 1) * DIFF_V] = (on * (1.0 - lam_init)).astype(BF16)


def _diff_attn(qkv, lam_p, norm_w, batch, seq, tq, lam_init):
    nq = seq // tq
    tokens = batch * seq
    return pl.pallas_call(
        functools.partial(_diff_kernel, tq=tq, lam_init=lam_init),
        grid=(batch, nq),
        in_specs=[pl.BlockSpec((tq, DIFF_WIDTH), lambda b, i: (b * nq + i, 0)),
                  pl.BlockSpec((seq, DIFF_WIDTH), lambda b, i: (b, 1)),
                  pl.BlockSpec((seq, DIFF_WIDTH), lambda b, i: (b, 2)),
                  pl.BlockSpec((SUBLANES, LANES), lambda b, i: (0, 0)),
                  pl.BlockSpec((1, DIFF_V), lambda b, i: (0, 0))],
        out_specs=pl.BlockSpec((tq, DIFF_WIDTH), lambda b, i: (b * nq + i, 0)),
        out_shape=jax.ShapeDtypeStruct((tokens, DIFF_WIDTH), BF16),
        scratch_shapes=[pltpu.VMEM((DIFF_HEADS, 2 * tq, LANES), F32),
                        pltpu.VMEM((DIFF_HEADS, 2 * tq, LANES), F32),
                        pltpu.VMEM((DIFF_HEADS, 2 * tq, DIFF_V), F32)],
        compiler_params=_params(("parallel", "parallel")),
        name="diff_attn",
    )(qkv, qkv, qkv, lam_p, norm_w)


def _outproj_kernel(x_ref, og_ref, os_ref, od_ref, w_ref, o_ref):
    acc = _dot(og_ref[...], w_ref[0:GDN_WIDTH, :])
    acc = acc + _dot(os_ref[...], w_ref[GDN_WIDTH:GDN_WIDTH + S5_WIDTH, :])
    acc = acc + _dot(od_ref[...], w_ref[GDN_WIDTH + S5_WIDTH:MIX_WIDTH, :])
    o_ref[...] = x_ref[...] + acc


def _outproj(x2, o_gdn, o_s5, o_diff, w_out, batch, seq, tm):
    nt = seq // tm
    row = lambda b, t: (b * nt + t, 0)
    return pl.pallas_call(
        _outproj_kernel,
        grid=(batch, nt),
        in_specs=[pl.BlockSpec((tm, D_MODEL), row),
                  pl.BlockSpec((tm, GDN_WIDTH), row),
                  pl.BlockSpec((tm, S5_WIDTH), row),
                  pl.BlockSpec((tm, DIFF_WIDTH), row),
                  pl.BlockSpec((MIX_WIDTH, D_MODEL), lambda b, t: (0, 0))],
        out_specs=pl.BlockSpec((tm, D_MODEL), row),
        out_shape=jax.ShapeDtypeStruct(x2.shape, F32),
        compiler_params=_params(("parallel", "parallel")),
        name="outproj",
    )(x2, o_gdn, o_s5, o_diff, w_out)


def _ffn_kernel(x_ref, nw_ref, wu_ref, cw_ref, cb_ref, wd_ref, fw_ref, o_ref, ub_ref, tail_ref, act_ref,
                *, tm, final_norm):
    @pl.when(pl.program_id(1) == 0)
    def _():
        tail_ref[...] = jnp.zeros_like(tail_ref)

    x = x_ref[...]
    ms = jnp.mean(x * x, axis=-1, keepdims=True)
    h = ((x * lax.rsqrt(ms + EPS)) * nw_ref[...]).astype(BF16)
    nslot = FFN_AHEAD + 1
    def cols(j):
        return [slice(part * D_FF + j * FFN_CHUNK, part * D_FF + (j + 1) * FFN_CHUNK) for part in range(2)]

    def up(j):
        for part, cs in enumerate(cols(j)):
            ub_ref[j % nslot, SUBLANES:SUBLANES + tm, part * FFN_CHUNK:(part + 1) * FFN_CHUNK] = (
                _dot(h, wu_ref[:, cs]))

    for j in range(min(FFN_AHEAD, FFN_NCHUNK)):
        up(j)
    for j in range(FFN_NCHUNK):
        slot = j % nslot
        if j + FFN_AHEAD < FFN_NCHUNK:
            up(j + FFN_AHEAD)
        ub_ref[slot, 0:SUBLANES, :] = tail_ref[j]
        tail_ref[j] = ub_ref[slot, tm:tm + SUBLANES, :]
        cv = []
        for part, cs in enumerate(cols(j)):
            acc = cb_ref[:, cs]
            for t in range(FFN_CONV):
                off = SUBLANES - (FFN_CONV - 1) + t
                acc = acc + (ub_ref[slot, off:off + tm, part * FFN_CHUNK:(part + 1) * FFN_CHUNK]
                             * cw_ref[t:t + 1, cs])
            cv.append(acc)
        act = _silu(cv[0]) * cv[1]
        act_ref[:, j * FFN_CHUNK:(j + 1) * FFN_CHUNK] = act.astype(BF16)
    y = x + _dot(act_ref[...], wd_ref[...])
    if final_norm:
        y = y * lax.rsqrt(jnp.mean(y * y, axis=-1, keepdims=True) + EPS) * fw_ref[...]
    o_ref[...] = y


def _ffn(x2, norm_w, w_up, conv_w, conv_b, w_down, final_w, batch, seq, tm, final_norm):
    nt = seq // tm
    row = lambda b, t: (b * nt + t, 0)
    c2 = lambda b, t: (0, 0)
    c3 = lambda b, t: (0, 0, 0)
    once = pl.Buffered(1)
    return pl.pallas_call(
        functools.partial(_ffn_kernel, tm=tm, final_norm=final_norm),
        grid=(batch, nt),
        in_specs=[pl.BlockSpec((tm, D_MODEL), row),
                  pl.BlockSpec((1, D_MODEL), c2),
                  pl.BlockSpec((D_MODEL, 2 * D_FF), c2, pipeline_mode=once),
                  pl.BlockSpec((SUBLANES, 2 * D_FF), c2),
                  pl.BlockSpec((1, 2 * D_FF), c2),
                  pl.BlockSpec((D_FF, D_MODEL), c2, pipeline_mode=once),
                  pl.BlockSpec((1, D_MODEL), c2)],
        out_specs=pl.BlockSpec((tm, D_MODEL), row),
        out_shape=jax.ShapeDtypeStruct(x2.shape, F32),
        scratch_shapes=[pltpu.VMEM((FFN_AHEAD + 1, tm + SUBLANES, 2 * FFN_CHUNK), F32),
                        pltpu.VMEM((FFN_NCHUNK, SUBLANES, 2 * FFN_CHUNK), F32),
                        pltpu.VMEM((tm, D_FF), BF16)],
        compiler_params=_params(("parallel", "arbitrary")),
        name="ffn",
    )(x2, norm_w, w_up, conv_w, conv_b, w_down, final_w)


def kernel(x, attn_norm_w, w_in, gdn_conv_w, gdn_a_log, gdn_dt_bias, gdn_norm_w, s5_lambda_re, s5_lambda_im, s5_log_dt, s5_b_re, s5_b_im, s5_c_re, s5_c_im, s5_d, s5_w_glu, s5_b_glu, s5_norm_w, diff_lambda_q1, diff_lambda_k1, diff_lambda_q2, diff_lambda_k2, diff_norm_w, w_out, ffn_norm_w, ffn_w_up, ffn_conv_w, ffn_conv_b, ffn_w_down, final_norm_w):
    batch, seq, _ = x.shape
    depth = w_in.shape[0]
    tm = min(1024, seq)
    tl = min(512, seq)
    tq = min(256, seq)
    ts = min(32, seq)
    assert seq % tm == 0 and seq % tl == 0 and seq % tq == 0 and seq % ts == 0
    assert tl % GDN_CHUNK == 0 and batch % SUBLANES == 0

    x2 = x.reshape(batch * seq, D_MODEL).astype(F32)
    o_b, o_su = 4 * GDN_WIDTH, 4 * GDN_WIDTH + 2 * GDN_HEADS
    o_dq = o_su + S5_WIDTH
    f32 = lambda t: t.astype(F32)
    w_gdn = jnp.pad(w_in[:, :, :o_su], ((0, 0), (0, 0), (0, LANES - 2 * GDN_HEADS))).astype(BF16)
    w_s5 = w_in[:, :, o_su:o_dq].astype(BF16)
    w_att = w_in[:, :, o_dq:].astype(BF16)
    gate_p = jnp.pad(jnp.stack([f32(gdn_a_log), f32(gdn_dt_bias)], axis=1),
                     ((0, 0), (0, SUBLANES - 2), (GDN_HEADS, LANES - 2 * GDN_HEADS)))
    a_bar, bre, bim, cre, cim = _s5_operators(s5_lambda_re, s5_lambda_im, s5_log_dt,
                                              s5_b_re, s5_b_im, s5_c_re, s5_c_im)
    w_glu = s5_w_glu.astype(BF16)
    lam_p = jnp.pad(jnp.stack([f32(diff_lambda_q1), f32(diff_lambda_k1), f32(diff_lambda_q2),
                               f32(diff_lambda_k2)], axis=1),
                    ((0, 0), (0, SUBLANES - 4), (0, LANES - DIFF_DIM)))
    w_o = w_out.astype(BF16)
    w_up = ffn_w_up.astype(BF16)
    w_down = ffn_w_down.astype(BF16)
    conv_w = jnp.pad(f32(ffn_conv_w), ((0, 0), (0, SUBLANES - FFN_CONV), (0, 0)))
    for i in range(depth):
        inf, s_u, dqkv = _inproj(x2, attn_norm_w[i][None, :], w_gdn[i], w_s5[i], w_att[i], batch, seq, tm)
        o_gdn = _gdn(inf, gdn_conv_w[i], gate_p[i], f32(gdn_norm_w[i][None, :]), batch, seq, tl)
        o_s5 = _s5(s_u.reshape(batch, seq, S5_WIDTH), a_bar[i], bre[i], bim[i], cre[i], cim[i],
                   s5_d[i][None, :], w_glu[i], s5_b_glu[i][None, :], s5_norm_w[i][None, :], batch, seq, ts)
        lam_init = 0.8 - 0.6 * math.exp(-0.3 * i)
        o_diff = _diff_attn(dqkv, lam_p[i], diff_norm_w[i][None, :], batch, seq, tq, lam_init)
        x2 = _outproj(x2, o_gdn, o_s5.reshape(batch * seq, S5_WIDTH), o_diff, w_o[i], batch, seq, tm)
        x2 = _ffn(x2, ffn_norm_w[i][None, :], w_up[i], conv_w[i], ffn_conv_b[i][None, :], w_down[i],
                  final_norm_w[None, :], batch, seq, tm, final_norm=(i == depth - 1))
    return x2.reshape(batch, seq, D_MODEL)
```

```python
import functools
import math

import jax
import jax.numpy as jnp
from jax import lax
from jax.experimental import pallas as pl
from jax.experimental.pallas import tpu as pltpu

F32 = jnp.float32
BF16 = jnp.bfloat16
HIGHEST = lax.Precision.HIGHEST

D_MODEL = 1024
EPS = 1e-6
GDN_HEADS = 4
GDN_DIM = 128
GDN_WIDTH = GDN_HEADS * GDN_DIM
GDN_CONV = 4
GDN_CHUNK = 64
GDN_INV_BLOCK = 16
S5_GROUP = 16
S5_GROUPS = 32
S5_WIDTH = S5_GROUPS * S5_GROUP
S5_STATE = 64
S5_NSTATE = S5_GROUPS * S5_STATE
S5_HALF_IN = S5_WIDTH // 2
S5_HALF_ST = S5_NSTATE // 2
DIFF_HEADS = 4
DIFF_DIM = 64
DIFF_V = 2 * DIFF_DIM
DIFF_WIDTH = DIFF_HEADS * DIFF_V
MIX_WIDTH = GDN_WIDTH + S5_WIDTH + DIFF_WIDTH
D_FF = 2816
FFN_CONV = 3
FFN_CHUNK = 256
FFN_NCHUNK = D_FF // FFN_CHUNK
FFN_AHEAD = 2

LANES = 128
SUBLANES = 8
VMEM_LIMIT = 56 * 1024 * 1024

INF_QKV = 3 * GDN_WIDTH
INF_WIDTH = INF_QKV + GDN_WIDTH + LANES


def _dot(a, b):
    return jnp.dot(a, b, preferred_element_type=F32)


def _dot_nt(a, b, precision=None):
    return lax.dot_general(a, b, (((1,), (1,)), ((), ())), precision=precision,
                           preferred_element_type=F32)


def _dot_tn(a, b):
    return lax.dot_general(a, b, (((0,), (0,)), ((), ())), preferred_element_type=F32)


def _sigmoid(x):
    return 0.5 + 0.5 * jnp.tanh(0.5 * x)


def _silu(x):
    h = 0.5 * x
    return h + h * jnp.tanh(h)


def _params(sem):
    return pltpu.CompilerParams(dimension_semantics=sem, vmem_limit_bytes=VMEM_LIMIT)


def _inproj_kernel(x_ref, nw_ref, wf_ref, ws_ref, wa_ref, of_ref, os_ref, oa_ref):
    x = x_ref[...]
    ms = jnp.mean(x * x, axis=-1, keepdims=True)
    h = ((x * lax.rsqrt(ms + EPS)) * nw_ref[...]).astype(BF16)
    of_ref[...] = _dot(h, wf_ref[...])
    os_ref[...] = _dot(h, ws_ref[...])
    oa_ref[...] = _dot(h, wa_ref[...]).astype(BF16)


def _inproj(x2, nw, wf, ws, wa, batch, seq, tm):
    nt = seq // tm
    tokens = batch * seq
    row = lambda b, t: (b * nt + t, 0)
    const = lambda b, t: (0, 0)
    return pl.pallas_call(
        _inproj_kernel,
        grid=(batch, nt),
        in_specs=[pl.BlockSpec((tm, D_MODEL), row),
                  pl.BlockSpec((1, D_MODEL), const),
                  pl.BlockSpec((D_MODEL, INF_WIDTH), const),
                  pl.BlockSpec((D_MODEL, S5_WIDTH), const),
                  pl.BlockSpec((D_MODEL, 3 * DIFF_WIDTH), const)],
        out_specs=[pl.BlockSpec((tm, INF_WIDTH), row),
                   pl.BlockSpec((tm, S5_WIDTH), row),
                   pl.BlockSpec((tm, 3 * DIFF_WIDTH), row)],
        out_shape=[jax.ShapeDtypeStruct((tokens, INF_WIDTH), F32),
                   jax.ShapeDtypeStruct((tokens, S5_WIDTH), F32),
                   jax.ShapeDtypeStruct((tokens, 3 * DIFF_WIDTH), BF16)],
        compiler_params=_params(("parallel", "parallel")),
        name="inproj",
    )(x2, nw, wf, ws, wa)


def _unit_lower_inverse(mats, row, col):
    shift = int(math.log2(GDN_INV_BLOCK))
    same = (row >> shift) == (col >> shift)
    eye = (row == col).astype(F32)
    mm = lambda ps, qs: [_dot(p, q) for p, q in zip(ps, qs)]
    add = lambda ps, qs: [p + q for p, q in zip(ps, qs)]
    b16 = lambda ps: [p.astype(BF16) for p in ps]
    ad = [jnp.where(same, a, 0.0) for a in mats]
    ao = b16([jnp.where(same, 0.0, a) for a in mats])
    p = [eye - a for a in ad]
    x = b16(ad)
    for _ in range(shift - 1):
        x = b16(mm(x, x))
        p = add(p, mm(b16(p), x))
    td = b16(p)
    n1 = mm(td, ao)
    n1b = b16(n1)
    n2 = mm(n1b, n1b)
    n3 = mm(n1b, b16(n2))
    return mm(b16([eye - a + b - d for a, b, d in zip(n1, n2, n3)]), td)


def _gdn_kernel(qkv_ref, z_ref, ba_ref, cw_ref, gp_ref, nw_ref, o_ref, xs_ref, q_s, k_s, v_s, st_ref,
                *, tl):
    c = GDN_CHUNK
    ti = pl.program_id(1)

    @pl.when(ti == 0)
    def _():
        xs_ref[0:SUBLANES, :] = jnp.zeros((SUBLANES, INF_QKV), F32)
        st_ref[...] = jnp.zeros_like(st_ref)

    xs_ref[SUBLANES:SUBLANES + tl, :] = qkv_ref[...]

    def conv_head(h):
        for part, dst in enumerate((q_s, k_s, v_s)):
            lo = part * GDN_WIDTH + h * GDN_DIM
            acc = jnp.zeros((tl, GDN_DIM), F32)
            for j in range(GDN_CONV):
                off = SUBLANES - (GDN_CONV - 1) + j
                acc = acc + xs_ref[off:off + tl, lo:lo + GDN_DIM] * cw_ref[j:j + 1, lo:lo + GDN_DIM]
            y = _silu(acc)
            if part < 2:
                y = y * lax.rsqrt(jnp.sum(y * y, axis=-1, keepdims=True) + 1e-6)
            if part == 0:
                y = y * (GDN_DIM ** -0.5)
            dst[h] = y

    row = lax.broadcasted_iota(jnp.int32, (c, c), 0)
    col = lax.broadcasted_iota(jnp.int32, (c, c), 1)
    incl = row >= col
    strict = row > col
    tri = incl.astype(F32)
    sel_r = lax.broadcasted_iota(jnp.int32, (SUBLANES, LANES), 0)
    sel_c = lax.broadcasted_iota(jnp.int32, (SUBLANES, LANES), 1)
    sel = (sel_c == sel_r).astype(F32)

    nc = tl // c
    nh = GDN_HEADS

    def rows(ci):
        return slice(ci * c, (ci + 1) * c)

    def gates():
        ba = ba_ref[...]
        sp = ba + gp_ref[1:2, :]
        softplus = jnp.maximum(sp, 0.0) + math.log(2.0) * jnp.log2(
            1.0 + jnp.exp2(-math.log2(math.e) * jnp.abs(sp)))
        g_all = -jnp.exp(gp_ref[0:1, :]) * softplus
        beta_all = _sigmoid(ba)
        is_beta = lax.broadcasted_iota(jnp.int32, (c, LANES), 1) < nh
        grow, bgcol = [], []
        for ci in range(nc):
            gc_c = jnp.dot(tri, g_all[rows(ci), :], precision=HIGHEST, preferred_element_type=F32)
            bgcol.append(jnp.where(is_beta, beta_all[rows(ci), :], gc_c))
            grow.append(_dot_nt(sel, gc_c, precision=HIGHEST))
        return grow, bgcol

    def chunk_operators(units):
        q = [q_s[h, rows(ci), :] for ci, h in units]
        k = [k_s[h, rows(ci), :] for ci, h in units]
        v = [v_s[h, rows(ci), :] for ci, h in units]
        beta = [bgcol[ci][:, h:h + 1] for ci, h in units]
        gc = [bgcol[ci][:, nh + h:nh + h + 1] for ci, h in units]
        gr = [grow[ci][nh + h:nh + h + 1, :] for ci, h in units]
        decay = [jnp.where(incl, jnp.exp(jnp.where(incl, a - b, 0.0)), 0.0) for a, b in zip(gc, gr)]
        egc = [jnp.exp(a) for a in gc]
        kb = [a * b for a, b in zip(k, beta)]
        kbf = [a.astype(BF16) for a in k]
        kk = [_dot_nt(a.astype(BF16), b) for a, b in zip(kb, kbf)]
        qk = [_dot_nt(a.astype(BF16), b) for a, b in zip(q, kbf)]
        a_low = [jnp.where(strict, a * d, 0.0) for a, d in zip(kk, decay)]
        attn = [jnp.where(incl, a * d, 0.0).astype(BF16) for a, d in zip(qk, decay)]
        t_inv = [t.astype(BF16) for t in _unit_lower_inverse(a_low, row, col)]
        u_mat = [_dot(t, (a * b).astype(BF16)) for t, a, b in zip(t_inv, v, beta)]
        w_mat = [_dot(t, (a * e).astype(BF16)).astype(BF16) for t, a, e in zip(t_inv, kb, egc)]
        qg = [(a * e).astype(BF16) for a, e in zip(q, egc)]
        g_last = [a[c - 1:c, :] for a in gc]
        kd = [(a * jnp.exp(gl - g)).astype(BF16) for a, gl, g in zip(k, g_last, gc)]
        kd_u = [_dot_tn(a, b.astype(BF16)) for a, b in zip(kd, u_mat)]
        kd_w = [_dot_tn(a, b).astype(BF16) for a, b in zip(kd, w_mat)]
        return dict(u=u_mat, w=w_mat, qg=qg, attn=attn, kd_u=kd_u, kd_w=kd_w,
                    eg=[jnp.exp(gl) for gl in g_last])

    for h in range(nh):
        conv_head(h)
    xs_ref[0:SUBLANES, :] = xs_ref[tl:tl + SUBLANES, :]
    grow, bgcol = gates()
    units = [(ci, h) for ci in range(nc) for h in range(nh)]
    ops = chunk_operators(units)

    state = [st_ref[h] for h in range(nh)]
    s_in = []
    for ci in range(nc):
        s_in.append([s.astype(BF16) for s in state])
        us = [ci * nh + h for h in range(nh)]
        state = [state[h] * ops['eg'][u] + ops['kd_u'][u] - _dot(ops['kd_w'][u], s_in[ci][h])
                 for h, u in enumerate(us)]
    for h in range(nh):
        st_ref[h] = state[h]

    ws = [_dot(w, s_in[ci][h]) for (ci, h), w in zip(units, ops['w'])]
    qs = [_dot(g, s_in[ci][h]) for (ci, h), g in zip(units, ops['qg'])]
    v_new = [(a - b).astype(BF16) for a, b in zip(ops['u'], ws)]
    o = [a + _dot(t, b) for a, t, b in zip(qs, ops['attn'], v_new)]
    for (ci, h), oo in zip(units, o):
        on = oo * lax.rsqrt(jnp.mean(oo * oo, axis=-1, keepdims=True) + EPS) * nw_ref[...]
        zz = z_ref[rows(ci), h * GDN_DIM:(h + 1) * GDN_DIM]
        o_ref[rows(ci), h * GDN_DIM:(h + 1) * GDN_DIM] = (on * _silu(zz)).astype(BF16)


def _gdn(inf, conv_w, gate_p, norm_w, batch, seq, tl):
    nt = seq // tl
    tokens = batch * seq
    row = lambda b, t: (b * nt + t, 0)
    const = lambda b, t: (0, 0)
    return pl.pallas_call(
        functools.partial(_gdn_kernel, tl=tl),
        grid=(batch, nt),
        in_specs=[pl.BlockSpec((tl, INF_QKV), row),
                  pl.BlockSpec((tl, GDN_WIDTH), lambda b, t: (b * nt + t, INF_QKV // GDN_WIDTH)),
                  pl.BlockSpec((tl, LANES), lambda b, t: (b * nt + t, (INF_QKV + GDN_WIDTH) // LANES)),
                  pl.BlockSpec((GDN_CONV, INF_QKV), const),
                  pl.BlockSpec((SUBLANES, LANES), const),
                  pl.BlockSpec((1, GDN_DIM), const)],
        out_specs=pl.BlockSpec((tl, GDN_WIDTH), row),
        out_shape=jax.ShapeDtypeStruct((tokens, GDN_WIDTH), BF16),
        scratch_shapes=[pltpu.VMEM((tl + SUBLANES, INF_QKV), F32),
                        pltpu.VMEM((GDN_HEADS, tl, GDN_DIM), F32),
                        pltpu.VMEM((GDN_HEADS, tl, GDN_DIM), F32),
                        pltpu.VMEM((GDN_HEADS, tl, GDN_DIM), F32),
                        pltpu.VMEM((GDN_HEADS, GDN_DIM, GDN_DIM), F32)],
        compiler_params=_params(("parallel", "arbitrary")),
        name="gdn",
    )(inf, inf, inf, conv_w, gate_p, norm_w)


S5_SCAN_LANES = 512


def _gelu_tanh(x):
    return 0.5 * x * (1.0 + jnp.tanh(math.sqrt(2.0 / math.pi) * (x + 0.044715 * (x * x * x))))


def _s5_kernel(u_ref, a_ref, bre_ref, bim_ref, cre_ref, cim_ref, d_ref, wg_ref, bg_ref, nw_ref, o_ref,
               sre_ref, sim_ref, *, batch, tl):
    rows = batch * tl

    @pl.when(pl.program_id(0) == 0)
    def _():
        sre_ref[0:batch, :] = jnp.zeros((batch, S5_NSTATE), F32)
        sim_ref[0:batch, :] = jnp.zeros((batch, S5_NSTATE), F32)

    u_bm = u_ref[...].reshape(rows, S5_WIDTH)
    r_i = lax.broadcasted_iota(jnp.int32, (rows, rows), 0)
    c_i = lax.broadcasted_iota(jnp.int32, (rows, rows), 1)
    to_tm = (c_i == (r_i % batch) * tl + r_i // batch).astype(BF16)
    to_bm = (c_i == (r_i % tl) * batch + r_i // tl).astype(BF16)
    u_hi = u_bm.astype(BF16)
    u_lo = (u_bm - u_hi.astype(F32)).astype(BF16)
    u_hi_tm = _dot(to_tm, u_hi)
    u = u_hi_tm + _dot(to_tm, u_lo)
    ub = u_hi_tm.astype(BF16)
    for hf in range(2):
        cin = slice(hf * S5_HALF_IN, (hf + 1) * S5_HALF_IN)
        cst = slice(hf * S5_HALF_ST, (hf + 1) * S5_HALF_ST)
        sre_ref[batch:batch + rows, cst] = _dot(ub[:, cin], bre_ref[hf])
        sim_ref[batch:batch + rows, cst] = _dot(ub[:, cin], bim_ref[hf])

    def scan(lc):
        cols = slice(lc * S5_SCAN_LANES, (lc + 1) * S5_SCAN_LANES)
        a_re = jnp.broadcast_to(a_ref[0:1, cols], (batch, S5_SCAN_LANES))
        a_im = jnp.broadcast_to(a_ref[1:2, cols], (batch, S5_SCAN_LANES))
        s_re, s_im = sre_ref[0:batch, cols], sim_ref[0:batch, cols]
        for t in range(tl):
            r = slice((t + 1) * batch, (t + 2) * batch)
            s_re, s_im = (a_re * s_re - a_im * s_im + sre_ref[r, cols],
                          a_re * s_im + a_im * s_re + sim_ref[r, cols])
            sre_ref[r, cols] = s_re
            sim_ref[r, cols] = s_im
        sre_ref[0:batch, cols] = s_re
        sim_ref[0:batch, cols] = s_im

    ys = []
    chunks_per_half = S5_HALF_ST // S5_SCAN_LANES
    for hf in range(2):
        for lc in range(hf * chunks_per_half, (hf + 1) * chunks_per_half):
            scan(lc)
        cst = slice(hf * S5_HALF_ST, (hf + 1) * S5_HALF_ST)
        s_re = sre_ref[batch:batch + rows, cst].astype(BF16)
        s_im = sim_ref[batch:batch + rows, cst].astype(BF16)
        ys.append(_dot(s_re, cre_ref[hf]) - _dot(s_im, cim_ref[hf]))
    y = jnp.concatenate(ys, axis=-1) + d_ref[...] * u
    y = _gelu_tanh(y)
    y = y * _sigmoid(_dot(y.astype(BF16), wg_ref[...]) + bg_ref[...])
    y = (y * lax.rsqrt(jnp.mean(y * y, axis=-1, keepdims=True) + EPS) * nw_ref[...]).astype(BF16)
    o_ref[...] = _dot(to_bm, y).astype(BF16).reshape(batch, tl, S5_WIDTH)


def _s5(u3, a_bar, bre, bim, cre, cim, d_skip, w_glu, b_glu, norm_w, batch, seq, tl):
    rows = batch * tl
    blk = lambda t: (0, t, 0)
    c2 = lambda t: (0, 0)
    c3 = lambda t: (0, 0, 0)
    return pl.pallas_call(
        functools.partial(_s5_kernel, batch=batch, tl=tl),
        grid=(seq // tl,),
        in_specs=[pl.BlockSpec((batch, tl, S5_WIDTH), blk),
                  pl.BlockSpec((SUBLANES, S5_NSTATE), c2),
                  pl.BlockSpec((2, S5_HALF_IN, S5_HALF_ST), c3),
                  pl.BlockSpec((2, S5_HALF_IN, S5_HALF_ST), c3),
                  pl.BlockSpec((2, S5_HALF_ST, S5_HALF_IN), c3),
                  pl.BlockSpec((2, S5_HALF_ST, S5_HALF_IN), c3),
                  pl.BlockSpec((1, S5_WIDTH), c2),
                  pl.BlockSpec((S5_WIDTH, S5_WIDTH), c2),
                  pl.BlockSpec((1, S5_WIDTH), c2),
                  pl.BlockSpec((1, S5_WIDTH), c2)],
        out_specs=pl.BlockSpec((batch, tl, S5_WIDTH), blk),
        out_shape=jax.ShapeDtypeStruct((batch, seq, S5_WIDTH), BF16),
        scratch_shapes=[pltpu.VMEM((batch + rows, S5_NSTATE), F32),
                        pltpu.VMEM((batch + rows, S5_NSTATE), F32)],
        compiler_params=_params(("arbitrary",)),
        name="s5",
    )(u3, a_bar, bre, bim, cre, cim, d_skip, w_glu, b_glu, norm_w)


def _s5_operators(lam_re, lam_im, log_dt, b_re, b_im, c_re, c_im):
    depth = lam_re.shape[0]
    l_re = jnp.minimum(lam_re.astype(F32), -1e-4)
    l_im = lam_im.astype(F32)
    dt = jnp.exp(log_dt.astype(F32))[..., None]
    mag = jnp.exp(l_re * dt)
    a_re = mag * jnp.cos(l_im * dt)
    a_im = mag * jnp.sin(l_im * dt)
    den = l_re * l_re + l_im * l_im
    f_re = ((a_re - 1.0) * l_re + a_im * l_im) / den
    f_im = (a_im * l_re - (a_re - 1.0) * l_im) / den
    bb_re = f_re[..., None] * b_re.astype(F32) - f_im[..., None] * b_im.astype(F32)
    bb_im = f_re[..., None] * b_im.astype(F32) + f_im[..., None] * b_re.astype(F32)
    a_bar = jnp.pad(jnp.stack([a_re.reshape(depth, -1), a_im.reshape(depth, -1)], axis=1),
                    ((0, 0), (0, SUBLANES - 2), (0, 0)))
    eye = jnp.eye(S5_GROUPS, dtype=F32)

    def b_op(t):
        full = jnp.einsum('dgph,gk->dghkp', t, eye).reshape(depth, S5_WIDTH, S5_NSTATE)
        return jnp.stack([full[:, :S5_HALF_IN, :S5_HALF_ST], full[:, S5_HALF_IN:, S5_HALF_ST:]],
                         axis=1).astype(BF16)

    def c_op(t):
        full = jnp.einsum('dghp,gk->dgpkh', t, eye).reshape(depth, S5_NSTATE, S5_WIDTH)
        return jnp.stack([full[:, :S5_HALF_ST, :S5_HALF_IN], full[:, S5_HALF_ST:, S5_HALF_IN:]],
                         axis=1).astype(BF16)

    return a_bar, b_op(bb_re), b_op(bb_im), c_op(c_re.astype(F32)), c_op(c_im.astype(F32))


ATT_ROWS = 32


def _diff_kernel(q_ref, k_ref, v_ref, lp_ref, nw_ref, o_ref, m_ref, l_ref, acc_ref, *, tq, lam_init):
    qi = pl.program_id(1)
    nh = DIFF_HEADS
    lane = lax.broadcasted_iota(jnp.int32, (tq, DIFF_V), 1)
    scale = DIFF_DIM ** -0.5 * math.log2(math.e)
    qs = []
    for h in range(nh):
        q = q_ref[:, h * DIFF_V:(h + 1) * DIFF_V]
        zero = jnp.zeros_like(q)
        q0 = jnp.where(lane < DIFF_DIM, q, zero)
        q1 = jnp.where(lane < DIFF_DIM, zero, q)
        qs.append((jnp.concatenate([q0, q1], axis=0).astype(F32) * scale).astype(BF16))

    m_ref[...] = jnp.full_like(m_ref, -jnp.inf)
    l_ref[...] = jnp.zeros_like(l_ref)
    acc_ref[...] = jnp.zeros_like(acc_ref)

    def update(kb, masked, nk=1):
        tk = nk * tq
        r = pl.ds(pl.multiple_of(kb * tq, tq), tk)
        hs = range(nh)
        kk = [k_ref[r, h * DIFF_V:(h + 1) * DIFF_V] for h in hs]
        vv = [v_ref[r, h * DIFF_V:(h + 1) * DIFF_V] for h in hs]
        s = {0: _dot_nt(qs[0], kk[0])}
        for h in hs:
            if h + 1 < nh:
                s[h + 1] = _dot_nt(qs[h + 1], kk[h + 1])
            pg, ag = [], []
            for g in range(2 * tq // ATT_ROWS):
                r0 = g * ATT_ROWS
                sg = s[h][r0:r0 + ATT_ROWS, :]
                if masked:
                    rr = lax.broadcasted_iota(jnp.int32, (ATT_ROWS, tk), 0) + (r0 % tq)
                    cc = lax.broadcasted_iota(jnp.int32, (ATT_ROWS, tk), 1) - (nk - 1) * tq
                    sg = jnp.where(rr >= cc, sg, -jnp.inf)
                tiles = [sg[:, j * LANES:(j + 1) * LANES] for j in range(tk // LANES)]
                m_cur = jnp.max(functools.reduce(jnp.maximum, tiles), axis=-1, keepdims=True)
                m_prev = m_ref[h, r0:r0 + ATT_ROWS, :]
                m_new = jnp.maximum(m_prev, m_cur)
                a = jnp.exp2(m_prev - m_new)
                p = [jnp.exp2(t - m_new) for t in tiles]
                l_ref[h, r0:r0 + ATT_ROWS, :] = a * l_ref[h, r0:r0 + ATT_ROWS, :] + functools.reduce(jnp.add, p)
                m_ref[h, r0:r0 + ATT_ROWS, :] = m_new
                pg.append(jnp.concatenate([t.astype(BF16) for t in p], axis=-1))
                ag.append(a)
            pv = _dot(jnp.concatenate(pg, axis=0), vv[h])
            acc_ref[h] = jnp.concatenate(ag, axis=0) * acc_ref[h] + pv

    def pair(i, carry):
        update(2 * i, False, nk=2)
        return carry

    lax.fori_loop(0, qi // 2, pair, 0)

    @pl.when(qi % 2 == 1)
    def _():
        update(qi - 1, False)

    update(qi, True)

    lp = lp_ref[...]
    lam = (jnp.exp(jnp.sum(lp[0:1, :] * lp[1:2, :], axis=-1, keepdims=True))
           - jnp.exp(jnp.sum(lp[2:3, :] * lp[3:4, :], axis=-1, keepdims=True)) + lam_init)
    for h in range(nh):
        o = acc_ref[h] / jnp.sum(l_ref[h], axis=-1, keepdims=True)
        o = o[:tq, :] - lam * o[tq:, :]
        on = o * lax.rsqrt(jnp.mean(o * o, axis=-1, keepdims=True) + EPS) * nw_ref[...]
        o_ref[:, h * DIFF_V:(h + 1) * DIFF_V] = (on * (1.0 - lam_init)).astype(BF16)


def _diff_attn(qkv, lam_p, norm_w, batch, seq, tq, lam_init):
    nq = seq // tq
    tokens = batch * seq
    return pl.pallas_call(
        functools.partial(_diff_kernel, tq=tq, lam_init=lam_init),
        grid=(batch, nq),
        in_specs=[pl.BlockSpec((tq, DIFF_WIDTH), lambda b, i: (b * nq + i, 0)),
                  pl.BlockSpec((seq, DIFF_WIDTH), lambda b, i: (b, 1)),
                  pl.BlockSpec((seq, DIFF_WIDTH), lambda b, i: (b, 2)),
                  pl.BlockSpec((SUBLANES, LANES), lambda b, i: (0, 0)),
                  pl.BlockSpec((1, DIFF_V), lambda b, i: (0, 0))],
        out_specs=pl.BlockSpec((tq, DIFF_WIDTH), lambda b, i: (b * nq + i, 0)),
        out_shape=jax.ShapeDtypeStruct((tokens, DIFF_WIDTH), BF16),
        scratch_shapes=[pltpu.VMEM((DIFF_HEADS, 2 * tq, LANES), F32),
                        pltpu.VMEM((DIFF_HEADS, 2 * tq, LANES), F32),
                        pltpu.VMEM((DIFF_HEADS, 2 * tq, DIFF_V), F32)],
        compiler_params=_params(("parallel", "parallel")),
        name="diff_attn",
    )(qkv, qkv, qkv, lam_p, norm_w)


def _outproj_kernel(x_ref, og_ref, os_ref, od_ref, w_ref, o_ref):
    acc = _dot(og_ref[...], w_ref[0:GDN_WIDTH, :])
    acc = acc + _dot(os_ref[...], w_ref[GDN_WIDTH:GDN_WIDTH + S5_WIDTH, :])
    acc = acc + _dot(od_ref[...], w_ref[GDN_WIDTH + S5_WIDTH:MIX_WIDTH, :])
    o_ref[...] = x_ref[...] + acc


def _outproj(x2, o_gdn, o_s5, o_diff, w_out, batch, seq, tm):
    nt = seq // tm
    row = lambda b, t: (b * nt + t, 0)
    return pl.pallas_call(
        _outproj_kernel,
        grid=(batch, nt),
        in_specs=[pl.BlockSpec((tm, D_MODEL), row),
                  pl.BlockSpec((tm, GDN_WIDTH), row),
                  pl.BlockSpec((tm, S5_WIDTH), row),
                  pl.BlockSpec((tm, DIFF_WIDTH), row),
                  pl.BlockSpec((MIX_WIDTH, D_MODEL), lambda b, t: (0, 0))],
        out_specs=pl.BlockSpec((tm, D_MODEL), row),
        out_shape=jax.ShapeDtypeStruct(x2.shape, F32),
        compiler_params=_params(("parallel", "parallel")),
        name="outproj",
    )(x2, o_gdn, o_s5, o_diff, w_out)


def _ffn_kernel(x_ref, nw_ref, wu_ref, cw_ref, cb_ref, wd_ref, fw_ref, o_ref, ub_ref, tail_ref, act_ref,
                *, tm, final_norm):
    @pl.when(pl.program_id(1) == 0)
    def _():
        tail_ref[...] = jnp.zeros_like(tail_ref)

    x = x_ref[...]
    ms = jnp.mean(x * x, axis=-1, keepdims=True)
    h = ((x * lax.rsqrt(ms + EPS)) * nw_ref[...]).astype(BF16)
    nslot = FFN_AHEAD + 1
    def cols(j):
        return [slice(part * D_FF + j * FFN_CHUNK, part * D_FF + (j + 1) * FFN_CHUNK) for part in range(2)]

    def up(j):
        for part, cs in enumerate(cols(j)):
            ub_ref[j % nslot, SUBLANES:SUBLANES + tm, part * FFN_CHUNK:(part + 1) * FFN_CHUNK] = (
                _dot(h, wu_ref[:, cs]))

    for j in range(min(FFN_AHEAD, FFN_NCHUNK)):
        up(j)
    for j in range(FFN_NCHUNK):
        slot = j % nslot
        if j + FFN_AHEAD < FFN_NCHUNK:
            up(j + FFN_AHEAD)
        ub_ref[slot, 0:SUBLANES, :] = tail_ref[j]
        tail_ref[j] = ub_ref[slot, tm:tm + SUBLANES, :]
        cv = []
        for part, cs in enumerate(cols(j)):
            acc = cb_ref[:, cs]
            for t in range(FFN_CONV):
                off = SUBLANES - (FFN_CONV - 1) + t
                acc = acc + (ub_ref[slot, off:off + tm, part * FFN_CHUNK:(part + 1) * FFN_CHUNK]
                             * cw_ref[t:t + 1, cs])
            cv.append(acc)
        act = _silu(cv[0]) * cv[1]
        act_ref[:, j * FFN_CHUNK:(j + 1) * FFN_CHUNK] = act.astype(BF16)
    y = x + _dot(act_ref[...], wd_ref[...])
    if final_norm:
        y = y * lax.rsqrt(jnp.mean(y * y, axis=-1, keepdims=True) + EPS) * fw_ref[...]
    o_ref[...] = y


def _ffn(x2, norm_w, w_up, conv_w, conv_b, w_down, final_w, batch, seq, tm, final_norm):
    nt = seq // tm
    row = lambda b, t: (b * nt + t, 0)
    c2 = lambda b, t: (0, 0)
    c3 = lambda b, t: (0, 0, 0)
    once = pl.Buffered(1)
    return pl.pallas_call(
        functools.partial(_ffn_kernel, tm=tm, final_norm=final_norm),
        grid=(batch, nt),
        in_specs=[pl.BlockSpec((tm, D_MODEL), row),
                  pl.BlockSpec((1, D_MODEL), c2),
                  pl.BlockSpec((D_MODEL, 2 * D_FF), c2, pipeline_mode=once),
                  pl.BlockSpec((SUBLANES, 2 * D_FF), c2),
                  pl.BlockSpec((1, 2 * D_FF), c2),
                  pl.BlockSpec((D_FF, D_MODEL), c2, pipeline_mode=once),
                  pl.BlockSpec((1, D_MODEL), c2)],
        out_specs=pl.BlockSpec((tm, D_MODEL), row),
        out_shape=jax.ShapeDtypeStruct(x2.shape, F32),
        scratch_shapes=[pltpu.VMEM((FFN_AHEAD + 1, tm + SUBLANES, 2 * FFN_CHUNK), F32),
                        pltpu.VMEM((FFN_NCHUNK, SUBLANES, 2 * FFN_CHUNK), F32),
                        pltpu.VMEM((tm, D_FF), BF16)],
        compiler_params=_params(("parallel", "arbitrary")),
        name="ffn",
    )(x2, norm_w, w_up, conv_w, conv_b, w_down, final_w)


def kernel(x, attn_norm_w, w_in, gdn_conv_w, gdn_a_log, gdn_dt_bias, gdn_norm_w, s5_lambda_re, s5_lambda_im, s5_log_dt, s5_b_re, s5_b_im, s5_c_re, s5_c_im, s5_d, s5_w_glu, s5_b_glu, s5_norm_w, diff_lambda_q1, diff_lambda_k1, diff_lambda_q2, diff_lambda_k2, diff_norm_w, w_out, ffn_norm_w, ffn_w_up, ffn_conv_w, ffn_conv_b, ffn_w_down, final_norm_w):
    batch, seq, _ = x.shape
    depth = w_in.shape[0]
    tm = min(1024, seq)
    tl = min(512, seq)
    tq = min(256, seq)
    ts = min(32, seq)
    assert seq % tm == 0 and seq % tl == 0 and seq % tq == 0 and seq % ts == 0
    assert tl % GDN_CHUNK == 0 and batch % SUBLANES == 0

    x2 = x.reshape(batch * seq, D_MODEL).astype(F32)
    o_b, o_su = 4 * GDN_WIDTH, 4 * GDN_WIDTH + 2 * GDN_HEADS
    o_dq = o_su + S5_WIDTH
    f32 = lambda t: t.astype(F32)
    w_gdn = jnp.pad(w_in[:, :, :o_su], ((0, 0), (0, 0), (0, LANES - 2 * GDN_HEADS))).astype(BF16)
    w_s5 = w_in[:, :, o_su:o_dq].astype(BF16)
    w_att = w_in[:, :, o_dq:].astype(BF16)
    gate_p = jnp.pad(jnp.stack([f32(gdn_a_log), f32(gdn_dt_bias)], axis=1),
                     ((0, 0), (0, SUBLANES - 2), (GDN_HEADS, LANES - 2 * GDN_HEADS)))
    a_bar, bre, bim, cre, cim = _s5_operators(s5_lambda_re, s5_lambda_im, s5_log_dt,
                                              s5_b_re, s5_b_im, s5_c_re, s5_c_im)
    w_glu = s5_w_glu.astype(BF16)
    lam_p = jnp.pad(jnp.stack([f32(diff_lambda_q1), f32(diff_lambda_k1), f32(diff_lambda_q2),
                               f32(diff_lambda_k2)], axis=1),
                    ((0, 0), (0, SUBLANES - 4), (0, LANES - DIFF_DIM)))
    w_o = w_out.astype(BF16)
    w_up = ffn_w_up.astype(BF16)
    w_down = ffn_w_down.astype(BF16)
    conv_w = jnp.pad(f32(ffn_conv_w), ((0, 0), (0, SUBLANES - FFN_CONV), (0, 0)))
    for i in range(depth):
        inf, s_u, dqkv = _inproj(x2, attn_norm_w[i][None, :], w_gdn[i], w_s5[i], w_att[i], batch, seq, tm)
        o_gdn = _gdn(inf, gdn_conv_w[i], gate_p[i], f32(gdn_norm_w[i][None, :]), batch, seq, tl)
        o_s5 = _s5(s_u.reshape(batch, seq, S5_WIDTH), a_bar[i], bre[i], bim[i], cre[i], cim[i],
                   s5_d[i][None, :], w_glu[i], s5_b_glu[i][None, :], s5_norm_w[i][None, :], batch, seq, ts)
        lam_init = 0.8 - 0.6 * math.exp(-0.3 * i)
        o_diff = _diff_attn(dqkv, lam_p[i], diff_norm_w[i][None, :], batch, seq, tq, lam_init)
        x2 = _outproj(x2, o_gdn, o_s5.reshape(batch * seq, S5_WIDTH), o_diff, w_o[i], batch, seq, tm)
        x2 = _ffn(x2, ffn_norm_w[i][None, :], w_up[i], conv_w[i], ffn_conv_b[i][None, :], w_down[i],
                  final_norm_w[None, :], batch, seq, tm, final_norm=(i == depth - 1))
    return x2.reshape(batch, seq, D_MODEL)
```

```python
import functools
import math

import jax
import jax.numpy as jnp
from jax import lax
from jax.experimental import pallas as pl
from jax.experimental.pallas import tpu as pltpu

F32 = jnp.float32
BF16 = jnp.bfloat16
HIGHEST = lax.Precision.HIGHEST

D_MODEL = 1024
EPS = 1e-6
GDN_HEADS = 4
GDN_DIM = 128
GDN_WIDTH = GDN_HEADS * GDN_DIM
GDN_CONV = 4
GDN_CHUNK = 64
GDN_INV_BLOCK = 16
S5_GROUP = 16
S5_GROUPS = 32
S5_WIDTH = S5_GROUPS * S5_GROUP
S5_STATE = 64
S5_NSTATE = S5_GROUPS * S5_STATE
S5_HALF_IN = S5_WIDTH // 2
S5_HALF_ST = S5_NSTATE // 2
DIFF_HEADS = 4
DIFF_DIM = 64
DIFF_V = 2 * DIFF_DIM
DIFF_WIDTH = DIFF_HEADS * DIFF_V
MIX_WIDTH = GDN_WIDTH + S5_WIDTH + DIFF_WIDTH
D_FF = 2816
FFN_CONV = 3
FFN_CHUNK = 256
FFN_NCHUNK = D_FF // FFN_CHUNK
FFN_AHEAD = 2

LANES = 128
SUBLANES = 8
VMEM_LIMIT = 56 * 1024 * 1024

INF_QKV = 3 * GDN_WIDTH
INF_WIDTH = INF_QKV + GDN_WIDTH + LANES


def _dot(a, b):
    return jnp.dot(a, b, preferred_element_type=F32)


def _dot_nt(a, b, precision=None):
    return lax.dot_general(a, b, (((1,), (1,)), ((), ())), precision=precision,
                           preferred_element_type=F32)


def _dot_tn(a, b):
    return lax.dot_general(a, b, (((0,), (0,)), ((), ())), preferred_element_type=F32)


def _sigmoid(x):
    return 0.5 + 0.5 * jnp.tanh(0.5 * x)


def _silu(x):
    h = 0.5 * x
    return h + h * jnp.tanh(h)


def _params(sem):
    return pltpu.CompilerParams(dimension_semantics=sem, vmem_limit_bytes=VMEM_LIMIT)


def _inproj_kernel(x_ref, nw_ref, wf_ref, ws_ref, wa_ref, of_ref, os_ref, oa_ref):
    x = x_ref[...]
    ms = jnp.mean(x * x, axis=-1, keepdims=True)
    h = ((x * lax.rsqrt(ms + EPS)) * nw_ref[...]).astype(BF16)
    of_ref[...] = _dot(h, wf_ref[...])
    os_ref[...] = _dot(h, ws_ref[...])
    oa_ref[...] = _dot(h, wa_ref[...]).astype(BF16)


def _inproj(x2, nw, wf, ws, wa, batch, seq, tm):
    nt = seq // tm
    tokens = batch * seq
    row = lambda b, t: (b * nt + t, 0)
    const = lambda b, t: (0, 0)
    return pl.pallas_call(
        _inproj_kernel,
        grid=(batch, nt),
        in_specs=[pl.BlockSpec((tm, D_MODEL), row),
                  pl.BlockSpec((1, D_MODEL), const),
                  pl.BlockSpec((D_MODEL, INF_WIDTH), const),
                  pl.BlockSpec((D_MODEL, S5_WIDTH), const),
                  pl.BlockSpec((D_MODEL, 3 * DIFF_WIDTH), const)],
        out_specs=[pl.BlockSpec((tm, INF_WIDTH), row),
                   pl.BlockSpec((tm, S5_WIDTH), row),
                   pl.BlockSpec((tm, 3 * DIFF_WIDTH), row)],
        out_shape=[jax.ShapeDtypeStruct((tokens, INF_WIDTH), F32),
                   jax.ShapeDtypeStruct((tokens, S5_WIDTH), F32),
                   jax.ShapeDtypeStruct((tokens, 3 * DIFF_WIDTH), BF16)],
        compiler_params=_params(("parallel", "parallel")),
        name="inproj",
    )(x2, nw, wf, ws, wa)


def _unit_lower_inverse(mats, row, col):
    shift = int(math.log2(GDN_INV_BLOCK))
    same = (row >> shift) == (col >> shift)
    eye = (row == col).astype(F32)
    mm = lambda ps, qs: [_dot(p, q) for p, q in zip(ps, qs)]
    add = lambda ps, qs: [p + q for p, q in zip(ps, qs)]
    b16 = lambda ps: [p.astype(BF16) for p in ps]
    ad = [jnp.where(same, a, 0.0) for a in mats]
    ao = b16([jnp.where(same, 0.0, a) for a in mats])
    p = [eye - a for a in ad]
    x = b16(ad)
    for _ in range(shift - 1):
        x = b16(mm(x, x))
        p = add(p, mm(b16(p), x))
    td = b16(p)
    n1 = mm(td, ao)
    n1b = b16(n1)
    n2 = mm(n1b, n1b)
    n3 = mm(n1b, b16(n2))
    return mm(b16([eye - a + b - d for a, b, d in zip(n1, n2, n3)]), td)


def _gdn_kernel(qkv_ref, z_ref, ba_ref, cw_ref, gp_ref, nw_ref, o_ref, xs_ref, q_s, k_s, v_s, st_ref,
                *, tl):
    c = GDN_CHUNK
    ti = pl.program_id(1)

    @pl.when(ti == 0)
    def _():
        xs_ref[0:SUBLANES, :] = jnp.zeros((SUBLANES, INF_QKV), F32)
        st_ref[...] = jnp.zeros_like(st_ref)

    xs_ref[SUBLANES:SUBLANES + tl, :] = qkv_ref[...]

    def conv_head(h):
        for part, dst in enumerate((q_s, k_s, v_s)):
            lo = part * GDN_WIDTH + h * GDN_DIM
            acc = jnp.zeros((tl, GDN_DIM), F32)
            for j in range(GDN_CONV):
                off = SUBLANES - (GDN_CONV - 1) + j
                acc = acc + xs_ref[off:off + tl, lo:lo + GDN_DIM] * cw_ref[j:j + 1, lo:lo + GDN_DIM]
            y = _silu(acc)
            if part < 2:
                y = y * lax.rsqrt(jnp.sum(y * y, axis=-1, keepdims=True) + 1e-6)
            if part == 0:
                y = y * (GDN_DIM ** -0.5)
            dst[h] = y

    row = lax.broadcasted_iota(jnp.int32, (c, c), 0)
    col = lax.broadcasted_iota(jnp.int32, (c, c), 1)
    incl = row >= col
    strict = row > col
    tri = incl.astype(F32)
    sel_r = lax.broadcasted_iota(jnp.int32, (SUBLANES, LANES), 0)
    sel_c = lax.broadcasted_iota(jnp.int32, (SUBLANES, LANES), 1)
    sel = (sel_c == sel_r).astype(F32)

    nc = tl // c
    nh = GDN_HEADS

    def rows(ci):
        return slice(ci * c, (ci + 1) * c)

    def gates():
        ba = ba_ref[...]
        sp = ba + gp_ref[1:2, :]
        softplus = jnp.maximum(sp, 0.0) + math.log(2.0) * jnp.log2(
            1.0 + jnp.exp2(-math.log2(math.e) * jnp.abs(sp)))
        g_all = -jnp.exp(gp_ref[0:1, :]) * softplus
        beta_all = _sigmoid(ba)
        is_beta = lax.broadcasted_iota(jnp.int32, (c, LANES), 1) < nh
        grow, bgcol = [], []
        for ci in range(nc):
            gc_c = jnp.dot(tri, g_all[rows(ci), :], precision=HIGHEST, preferred_element_type=F32)
            bgcol.append(jnp.where(is_beta, beta_all[rows(ci), :], gc_c))
            grow.append(_dot_nt(sel, gc_c, precision=HIGHEST))
        return grow, bgcol

    def chunk_operators(units):
        q = [q_s[h, rows(ci), :] for ci, h in units]
        k = [k_s[h, rows(ci), :] for ci, h in units]
        v = [v_s[h, rows(ci), :] for ci, h in units]
        beta = [bgcol[ci][:, h:h + 1] for ci, h in units]
        gc = [bgcol[ci][:, nh + h:nh + h + 1] for ci, h in units]
        gr = [grow[ci][nh + h:nh + h + 1, :] for ci, h in units]
        decay = [jnp.where(incl, jnp.exp(jnp.where(incl, a - b, 0.0)), 0.0) for a, b in zip(gc, gr)]
        egc = [jnp.exp(a) for a in gc]
        kb = [a * b for a, b in zip(k, beta)]
        kbf = [a.astype(BF16) for a in k]
        kk = [_dot_nt(a.astype(BF16), b) for a, b in zip(kb, kbf)]
        qk = [_dot_nt(a.astype(BF16), b) for a, b in zip(q, kbf)]
        a_low = [jnp.where(strict, a * d, 0.0) for a, d in zip(kk, decay)]
        attn = [jnp.where(incl, a * d, 0.0).astype(BF16) for a, d in zip(qk, decay)]
        t_inv = [t.astype(BF16) for t in _unit_lower_inverse(a_low, row, col)]
        u_mat = [_dot(t, (a * b).astype(BF16)) for t, a, b in zip(t_inv, v, beta)]
        w_mat = [_dot(t, (a * e).astype(BF16)).astype(BF16) for t, a, e in zip(t_inv, kb, egc)]
        qg = [(a * e).astype(BF16) for a, e in zip(q, egc)]
        g_last = [a[c - 1:c, :] for a in gc]
        kd = [(a * jnp.exp(gl - g)).astype(BF16) for a, gl, g in zip(k, g_last, gc)]
        kd_u = [_dot_tn(a, b.astype(BF16)) for a, b in zip(kd, u_mat)]
        kd_w = [_dot_tn(a, b).astype(BF16) for a, b in zip(kd, w_mat)]
        return dict(u=u_mat, w=w_mat, qg=qg, attn=attn, kd_u=kd_u, kd_w=kd_w,
                    eg=[jnp.exp(gl) for gl in g_last])

    for h in range(nh):
        conv_head(h)
    xs_ref[0:SUBLANES, :] = xs_ref[tl:tl + SUBLANES, :]
    grow, bgcol = gates()
    units = [(ci, h) for ci in range(nc) for h in range(nh)]
    ops = chunk_operators(units)

    state = [st_ref[h] for h in range(nh)]
    s_in = []
    for ci in range(nc):
        s_in.append([s.astype(BF16) for s in state])
        us = [ci * nh + h for h in range(nh)]
        state = [state[h] * ops['eg'][u] + ops['kd_u'][u] - _dot(ops['kd_w'][u], s_in[ci][h])
                 for h, u in enumerate(us)]
    for h in range(nh):
        st_ref[h] = state[h]

    ws = [_dot(w, s_in[ci][h]) for (ci, h), w in zip(units, ops['w'])]
    qs = [_dot(g, s_in[ci][h]) for (ci, h), g in zip(units, ops['qg'])]
    v_new = [(a - b).astype(BF16) for a, b in zip(ops['u'], ws)]
    o = [a + _dot(t, b) for a, t, b in zip(qs, ops['attn'], v_new)]
    for (ci, h), oo in zip(units, o):
        on = oo * lax.rsqrt(jnp.mean(oo * oo, axis=-1, keepdims=True) + EPS) * nw_ref[...]
        zz = z_ref[rows(ci), h * GDN_DIM:(h + 1) * GDN_DIM]
        o_ref[rows(ci), h * GDN_DIM:(h + 1) * GDN_DIM] = (on * _silu(zz)).astype(BF16)


def _gdn(inf, conv_w, gate_p, norm_w, batch, seq, tl):
    nt = seq // tl
    tokens = batch * seq
    row = lambda b, t: (b * nt + t, 0)
    const = lambda b, t: (0, 0)
    return pl.pallas_call(
        functools.partial(_gdn_kernel, tl=tl),
        grid=(batch, nt),
        in_specs=[pl.BlockSpec((tl, INF_QKV), row),
                  pl.BlockSpec((tl, GDN_WIDTH), lambda b, t: (b * nt + t, INF_QKV // GDN_WIDTH)),
                  pl.BlockSpec((tl, LANES), lambda b, t: (b * nt + t, (INF_QKV + GDN_WIDTH) // LANES)),
                  pl.BlockSpec((GDN_CONV, INF_QKV), const),
                  pl.BlockSpec((SUBLANES, LANES), const),
                  pl.BlockSpec((1, GDN_DIM), const)],
        out_specs=pl.BlockSpec((tl, GDN_WIDTH), row),
        out_shape=jax.ShapeDtypeStruct((tokens, GDN_WIDTH), BF16),
        scratch_shapes=[pltpu.VMEM((tl + SUBLANES, INF_QKV), F32),
                        pltpu.VMEM((GDN_HEADS, tl, GDN_DIM), F32),
                        pltpu.VMEM((GDN_HEADS, tl, GDN_DIM), F32),
                        pltpu.VMEM((GDN_HEADS, tl, GDN_DIM), F32),
                        pltpu.VMEM((GDN_HEADS, GDN_DIM, GDN_DIM), F32)],
        compiler_params=_params(("parallel", "arbitrary")),
        name="gdn",
    )(inf, inf, inf, conv_w, gate_p, norm_w)


S5_SCAN_LANES = 512


def _gelu_tanh(x):
    return 0.5 * x * (1.0 + jnp.tanh(math.sqrt(2.0 / math.pi) * (x + 0.044715 * (x * x * x))))


def _s5_kernel(u_ref, a_ref, bre_ref, bim_ref, cre_ref, cim_ref, d_ref, wg_ref, bg_ref, nw_ref, o_ref,
               sre_ref, sim_ref, *, batch, tl):
    rows = batch * tl

    @pl.when(pl.program_id(0) == 0)
    def _():
        sre_ref[0:batch, :] = jnp.zeros((batch, S5_NSTATE), F32)
        sim_ref[0:batch, :] = jnp.zeros((batch, S5_NSTATE), F32)

    u_bm = u_ref[...].reshape(rows, S5_WIDTH)
    r_i = lax.broadcasted_iota(jnp.int32, (rows, rows), 0)
    c_i = lax.broadcasted_iota(jnp.int32, (rows, rows), 1)
    to_tm = (c_i == (r_i % batch) * tl + r_i // batch).astype(BF16)
    to_bm = (c_i == (r_i % tl) * batch + r_i // tl).astype(BF16)
    u_hi = u_bm.astype(BF16)
    u_lo = (u_bm - u_hi.astype(F32)).astype(BF16)
    u_hi_tm = _dot(to_tm, u_hi)
    u = u_hi_tm + _dot(to_tm, u_lo)
    ub = u_hi_tm.astype(BF16)
    for hf in range(2):
        cin = slice(hf * S5_HALF_IN, (hf + 1) * S5_HALF_IN)
        cst = slice(hf * S5_HALF_ST, (hf + 1) * S5_HALF_ST)
        sre_ref[batch:batch + rows, cst] = _dot(ub[:, cin], bre_ref[hf])
        sim_ref[batch:batch + rows, cst] = _dot(ub[:, cin], bim_ref[hf])

    def scan(lc):
        cols = slice(lc * S5_SCAN_LANES, (lc + 1) * S5_SCAN_LANES)
        a_re = jnp.broadcast_to(a_ref[0:1, cols], (batch, S5_SCAN_LANES))
        a_im = jnp.broadcast_to(a_ref[1:2, cols], (batch, S5_SCAN_LANES))
        s_re, s_im = sre_ref[0:batch, cols], sim_ref[0:batch, cols]
        for t in range(tl):
            r = slice((t + 1) * batch, (t + 2) * batch)
            s_re, s_im = (a_re * s_re - a_im * s_im + sre_ref[r, cols],
                          a_re * s_im + a_im * s_re + sim_ref[r, cols])
            sre_ref[r, cols] = s_re
            sim_ref[r, cols] = s_im
        sre_ref[0:batch, cols] = s_re
        sim_ref[0:batch, cols] = s_im

    ys = []
    chunks_per_half = S5_HALF_ST // S5_SCAN_LANES
    for hf in range(2):
        for lc in range(hf * chunks_per_half, (hf + 1) * chunks_per_half):
            scan(lc)
        cst = slice(hf * S5_HALF_ST, (hf + 1) * S5_HALF_ST)
        s_re = sre_ref[batch:batch + rows, cst].astype(BF16)
        s_im = sim_ref[batch:batch + rows, cst].astype(BF16)
        ys.append(_dot(s_re, cre_ref[hf]) - _dot(s_im, cim_ref[hf]))
    y = jnp.concatenate(ys, axis=-1) + d_ref[...] * u
    y = _gelu_tanh(y)
    y = y * _sigmoid(_dot(y.astype(BF16), wg_ref[...]) + bg_ref[...])
    y = (y * lax.rsqrt(jnp.mean(y * y, axis=-1, keepdims=True) + EPS) * nw_ref[...]).astype(BF16)
    o_ref[...] = _dot(to_bm, y).astype(BF16).reshape(batch, tl, S5_WIDTH)


def _s5(u3, a_bar, bre, bim, cre, cim, d_skip, w_glu, b_glu, norm_w, batch, seq, tl):
    rows = batch * tl
    blk = lambda t: (0, t, 0)
    c2 = lambda t: (0, 0)
    c3 = lambda t: (0, 0, 0)
    return pl.pallas_call(
        functools.partial(_s5_kernel, batch=batch, tl=tl),
        grid=(seq // tl,),
        in_specs=[pl.BlockSpec((batch, tl, S5_WIDTH), blk),
                  pl.BlockSpec((SUBLANES, S5_NSTATE), c2),
                  pl.BlockSpec((2, S5_HALF_IN, S5_HALF_ST), c3),
                  pl.BlockSpec((2, S5_HALF_IN, S5_HALF_ST), c3),
                  pl.BlockSpec((2, S5_HALF_ST, S5_HALF_IN), c3),
                  pl.BlockSpec((2, S5_HALF_ST, S5_HALF_IN), c3),
                  pl.BlockSpec((1, S5_WIDTH), c2),
                  pl.BlockSpec((S5_WIDTH, S5_WIDTH), c2),
                  pl.BlockSpec((1, S5_WIDTH), c2),
                  pl.BlockSpec((1, S5_WIDTH), c2)],
        out_specs=pl.BlockSpec((batch, tl, S5_WIDTH), blk),
        out_shape=jax.ShapeDtypeStruct((batch, seq, S5_WIDTH), BF16),
        scratch_shapes=[pltpu.VMEM((batch + rows, S5_NSTATE), F32),
                        pltpu.VMEM((batch + rows, S5_NSTATE), F32)],
        compiler_params=_params(("arbitrary",)),
        name="s5",
    )(u3, a_bar, bre, bim, cre, cim, d_skip, w_glu, b_glu, norm_w)


def _s5_operators(lam_re, lam_im, log_dt, b_re, b_im, c_re, c_im):
    depth = lam_re.shape[0]
    l_re = jnp.minimum(lam_re.astype(F32), -1e-4)
    l_im = lam_im.astype(F32)
    dt = jnp.exp(log_dt.astype(F32))[..., None]
    mag = jnp.exp(l_re * dt)
    a_re = mag * jnp.cos(l_im * dt)
    a_im = mag * jnp.sin(l_im * dt)
    den = l_re * l_re + l_im * l_im
    f_re = ((a_re - 1.0) * l_re + a_im * l_im) / den
    f_im = (a_im * l_re - (a_re - 1.0) * l_im) / den
    bb_re = f_re[..., None] * b_re.astype(F32) - f_im[..., None] * b_im.astype(F32)
    bb_im = f_re[..., None] * b_im.astype(F32) + f_im[..., None] * b_re.astype(F32)
    a_bar = jnp.pad(jnp.stack([a_re.reshape(depth, -1), a_im.reshape(depth, -1)], axis=1),
                    ((0, 0), (0, SUBLANES - 2), (0, 0)))
    eye = jnp.eye(S5_GROUPS, dtype=F32)

    def b_op(t):
        full = jnp.einsum('dgph,gk->dghkp', t, eye).reshape(depth, S5_WIDTH, S5_NSTATE)
        return jnp.stack([full[:, :S5_HALF_IN, :S5_HALF_ST], full[:, S5_HALF_IN:, S5_HALF_ST:]],
                         axis=1).astype(BF16)

    def c_op(t):
        full = jnp.einsum('dghp,gk->dgpkh', t, eye).reshape(depth, S5_NSTATE, S5_WIDTH)
        return jnp.stack([full[:, :S5_HALF_ST, :S5_HALF_IN], full[:, S5_HALF_ST:, S5_HALF_IN:]],
                         axis=1).astype(BF16)

    return a_bar, b_op(bb_re), b_op(bb_im), c_op(c_re.astype(F32)), c_op(c_im.astype(F32))


ATT_ROWS = 32


def _diff_kernel(q_ref, k_ref, v_ref, lp_ref, nw_ref, o_ref, m_ref, l_ref, acc_ref, *, tq, lam_init):
    qi = pl.program_id(1)
    nh = DIFF_HEADS
    lane = lax.broadcasted_iota(jnp.int32, (tq, DIFF_V), 1)
    scale = DIFF_DIM ** -0.5 * math.log2(math.e)
    qs = []
    for h in range(nh):
        q = q_ref[:, h * DIFF_V:(h + 1) * DIFF_V]
        zero = jnp.zeros_like(q)
        q0 = jnp.where(lane < DIFF_DIM, q, zero)
        q1 = jnp.where(lane < DIFF_DIM, zero, q)
        qs.append((jnp.concatenate([q0, q1], axis=0).astype(F32) * scale).astype(BF16))

    m_ref[...] = jnp.full_like(m_ref, -jnp.inf)
    l_ref[...] = jnp.zeros_like(l_ref)
    acc_ref[...] = jnp.zeros_like(acc_ref)

    def update(kb, masked, nk=1):
        tk = nk * tq
        r = pl.ds(pl.multiple_of(kb * tq, tq), tk)
        hs = range(nh)
        kk = [k_ref[r, h * DIFF_V:(h + 1) * DIFF_V] for h in hs]
        vv = [v_ref[r, h * DIFF_V:(h + 1) * DIFF_V] for h in hs]
        s = {0: _dot_nt(qs[0], kk[0])}
        for h in hs:
            if h + 1 < nh:
                s[h + 1] = _dot_nt(qs[h + 1], kk[h + 1])
            pg, ag = [], []
            for g in range(2 * tq // ATT_ROWS):
                r0 = g * ATT_ROWS
                sg = s[h][r0:r0 + ATT_ROWS, :]
                if masked:
                    rr = lax.broadcasted_iota(jnp.int32, (ATT_ROWS, tk), 0) + (r0 % tq)
                    cc = lax.broadcasted_iota(jnp.int32, (ATT_ROWS, tk), 1) - (nk - 1) * tq
                    sg = jnp.where(rr >= cc, sg, -jnp.inf)
                tiles = [sg[:, j * LANES:(j + 1) * LANES] for j in range(tk // LANES)]
                m_cur = jnp.max(functools.reduce(jnp.maximum, tiles), axis=-1, keepdims=True)
                m_prev = m_ref[h, r0:r0 + ATT_ROWS, :]
                m_new = jnp.maximum(m_prev, m_cur)
                a = jnp.exp2(m_prev - m_new)
                p = [jnp.exp2(t - m_new) for t in tiles]
                l_ref[h, r0:r0 + ATT_ROWS, :] = a * l_ref[h, r0:r0 + ATT_ROWS, :] + functools.reduce(jnp.add, p)
                m_ref[h, r0:r0 + ATT_ROWS, :] = m_new
                pg.append(jnp.concatenate([t.astype(BF16) for t in p], axis=-1))
                ag.append(a)
            pv = _dot(jnp.concatenate(pg, axis=0), vv[h])
            acc_ref[h] = jnp.concatenate(ag, axis=0) * acc_ref[h] + pv

    def pair(i, carry):
        update(2 * i, False, nk=2)
        return carry

    lax.fori_loop(0, qi // 2, pair, 0)

    @pl.when(qi % 2 == 1)
    def _():
        update(qi - 1, False)

    update(qi, True)

    lp = lp_ref[...]
    lam = (jnp.exp(jnp.sum(lp[0:1, :] * lp[1:2, :], axis=-1, keepdims=True))
           - jnp.exp(jnp.sum(lp[2:3, :] * lp[3:4, :], axis=-1, keepdims=True)) + lam_init)
    for h in range(nh):
        o = acc_ref[h] / jnp.sum(l_ref[h], axis=-1, keepdims=True)
        o = o[:tq, :] - lam * o[tq:, :]
        on = o * lax.rsqrt(jnp.mean(o * o, axis=-1, keepdims=True) + EPS) * nw_ref[...]
        o_ref[:, h * DIFF_V:(h + 1) * DIFF_V] = (on * (1.0 - lam_init)).astype(BF16)


def _diff_attn(qkv, lam_p, norm_w, batch, seq, tq, lam_init):
    nq = seq // tq
    tokens = batch * seq
    return pl.pallas_call(
        functools.partial(_diff_kernel, tq=tq, lam_init=lam_init),
        grid=(batch, nq),
        in_specs=[pl.BlockSpec((tq, DIFF_WIDTH), lambda b, i: (b * nq + i, 0)),
                  pl.BlockSpec((seq, DIFF_WIDTH), lambda b, i: (b, 1)),
                  pl.BlockSpec((seq, DIFF_WIDTH), lambda b, i: (b, 2)),
                  pl.BlockSpec((SUBLANES, LANES), lambda b, i: (0, 0)),
                  pl.BlockSpec((1, DIFF_V), lambda b, i: (0, 0))],
        out_specs=pl.BlockSpec((tq, DIFF_WIDTH), lambda b, i: (b * nq + i, 0)),
        out_shape=jax.ShapeDtypeStruct((tokens, DIFF_WIDTH), BF16),
        scratch_shapes=[pltpu.VMEM((DIFF_HEADS, 2 * tq, LANES), F32),
                        pltpu.VMEM((DIFF_HEADS, 2 * tq, LANES), F32),
                        pltpu.VMEM((DIFF_HEADS, 2 * tq, DIFF_V), F32)],
        compiler_params=_params(("parallel", "parallel")),
        name="diff_attn",
    )(qkv, qkv, qkv, lam_p, norm_w)


def _outproj_kernel(x_ref, og_ref, os_ref, od_ref, w_ref, o_ref):
    acc = _dot(og_ref[...], w_ref[0:GDN_WIDTH, :])
    acc = acc + _dot(os_ref[...], w_ref[GDN_WIDTH:GDN_WIDTH + S5_WIDTH, :])
    acc = acc + _dot(od_ref[...], w_ref[GDN_WIDTH + S5_WIDTH:MIX_WIDTH, :])
    o_ref[...] = x_ref[...] + acc


def _outproj(x2, o_gdn, o_s5, o_diff, w_out, batch, seq, tm):
    nt = seq // tm
    row = lambda b, t: (b * nt + t, 0)
    return pl.pallas_call(
        _outproj_kernel,
        grid=(batch, nt),
        in_specs=[pl.BlockSpec((tm, D_MODEL), row),
                  pl.BlockSpec((tm, GDN_WIDTH), row),
                  pl.BlockSpec((tm, S5_WIDTH), row),
                  pl.BlockSpec((tm, DIFF_WIDTH), row),
                  pl.BlockSpec((MIX_WIDTH, D_MODEL), lambda b, t: (0, 0))],
        out_specs=pl.BlockSpec((tm, D_MODEL), row),
        out_shape=jax.ShapeDtypeStruct(x2.shape, F32),
        compiler_params=_params(("parallel", "parallel")),
        name="outproj",
    )(x2, o_gdn, o_s5, o_diff, w_out)


def _ffn_kernel(x_ref, nw_ref, wu_ref, cw_ref, cb_ref, wd_ref, fw_ref, o_ref, ub_ref, tail_ref, act_ref,
                *, tm, final_norm):
    @pl.when(pl.program_id(1) == 0)
    def _():
        tail_ref[...] = jnp.zeros_like(tail_ref)

    x = x_ref[...]
    ms = jnp.mean(x * x, axis=-1, keepdims=True)
    h = ((x * lax.rsqrt(ms + EPS)) * nw_ref[...]).astype(BF16)
    nslot = FFN_AHEAD + 1
    def cols(j):
        return [slice(part * D_FF + j * FFN_CHUNK, part * D_FF + (j + 1) * FFN_CHUNK) for part in range(2)]

    def up(j):
        for part, cs in enumerate(cols(j)):
            ub_ref[j % nslot, SUBLANES:SUBLANES + tm, part * FFN_CHUNK:(part + 1) * FFN_CHUNK] = (
                _dot(h, wu_ref[:, cs]))

    for j in range(min(FFN_AHEAD, FFN_NCHUNK)):
        up(j)
    for j in range(FFN_NCHUNK):
        slot = j % nslot
        if j + FFN_AHEAD < FFN_NCHUNK:
            up(j + FFN_AHEAD)
        ub_ref[slot, 0:SUBLANES, :] = tail_ref[j]
        tail_ref[j] = ub_ref[slot, tm:tm + SUBLANES, :]
        cv = []
        for part, cs in enumerate(cols(j)):
            acc = cb_ref[:, cs]
            for t in range(FFN_CONV):
                off = SUBLANES - (FFN_CONV - 1) + t
                acc = acc + (ub_ref[slot, off:off + tm, part * FFN_CHUNK:(part + 1) * FFN_CHUNK]
                             * cw_ref[t:t + 1, cs])
            cv.append(acc)
        act = _silu(cv[0]) * cv[1]
        act_ref[:, j * FFN_CHUNK:(j + 1) * FFN_CHUNK] = act.astype(BF16)
    y = x + _dot(act_ref[...], wd_ref[...])
    if final_norm:
        y = y * lax.rsqrt(jnp.mean(y * y, axis=-1, keepdims=True) + EPS) * fw_ref[...]
    o_ref[...] = y


def _ffn(x2, norm_w, w_up, conv_w, conv_b, w_down, final_w, batch, seq, tm, final_norm):
    nt = seq // tm
    row = lambda b, t: (b * nt + t, 0)
    c2 = lambda b, t: (0, 0)
    c3 = lambda b, t: (0, 0, 0)
    once = pl.Buffered(1)
    return pl.pallas_call(
        functools.partial(_ffn_kernel, tm=tm, final_norm=final_norm),
        grid=(batch, nt),
        in_specs=[pl.BlockSpec((tm, D_MODEL), row),
                  pl.BlockSpec((1, D_MODEL), c2),
                  pl.BlockSpec((D_MODEL, 2 * D_FF), c2, pipeline_mode=once),
                  pl.BlockSpec((SUBLANES, 2 * D_FF), c2),
                  pl.BlockSpec((1, 2 * D_FF), c2),
                  pl.BlockSpec((D_FF, D_MODEL), c2, pipeline_mode=once),
                  pl.BlockSpec((1, D_MODEL), c2)],
        out_specs=pl.BlockSpec((tm, D_MODEL), row),
        out_shape=jax.ShapeDtypeStruct(x2.shape, F32),
        scratch_shapes=[pltpu.VMEM((FFN_AHEAD + 1, tm + SUBLANES, 2 * FFN_CHUNK), F32),
                        pltpu.VMEM((FFN_NCHUNK, SUBLANES, 2 * FFN_CHUNK), F32),
                        pltpu.VMEM((tm, D_FF), BF16)],
        compiler_params=_params(("parallel", "arbitrary")),
        name="ffn",
    )(x2, norm_w, w_up, conv_w, conv_b, w_down, final_w)


def kernel(x, attn_norm_w, w_in, gdn_conv_w, gdn_a_log, gdn_dt_bias, gdn_norm_w, s5_lambda_re, s5_lambda_im, s5_log_dt, s5_b_re, s5_b_im, s5_c_re, s5_c_im, s5_d, s5_w_glu, s5_b_glu, s5_norm_w, diff_lambda_q1, diff_lambda_k1, diff_lambda_q2, diff_lambda_k2, diff_norm_w, w_out, ffn_norm_w, ffn_w_up, ffn_conv_w, ffn_conv_b, ffn_w_down, final_norm_w):
    batch, seq, _ = x.shape
    depth = w_in.shape[0]
    tm = min(1024, seq)
    tl = min(512, seq)
    tq = min(512, seq)
    ts = min(32, seq)
    assert seq % tm == 0 and seq % tl == 0 and seq % tq == 0 and seq % ts == 0
    assert tl % GDN_CHUNK == 0 and batch % SUBLANES == 0

    x2 = x.reshape(batch * seq, D_MODEL).astype(F32)
    o_b, o_su = 4 * GDN_WIDTH, 4 * GDN_WIDTH + 2 * GDN_HEADS
    o_dq = o_su + S5_WIDTH
    f32 = lambda t: t.astype(F32)
    w_gdn = jnp.pad(w_in[:, :, :o_su], ((0, 0), (0, 0), (0, LANES - 2 * GDN_HEADS))).astype(BF16)
    w_s5 = w_in[:, :, o_su:o_dq].astype(BF16)
    w_att = w_in[:, :, o_dq:].astype(BF16)
    gate_p = jnp.pad(jnp.stack([f32(gdn_a_log), f32(gdn_dt_bias)], axis=1),
                     ((0, 0), (0, SUBLANES - 2), (GDN_HEADS, LANES - 2 * GDN_HEADS)))
    a_bar, bre, bim, cre, cim = _s5_operators(s5_lambda_re, s5_lambda_im, s5_log_dt,
                                              s5_b_re, s5_b_im, s5_c_re, s5_c_im)
    w_glu = s5_w_glu.astype(BF16)
    lam_p = jnp.pad(jnp.stack([f32(diff_lambda_q1), f32(diff_lambda_k1), f32(diff_lambda_q2),
                               f32(diff_lambda_k2)], axis=1),
                    ((0, 0), (0, SUBLANES - 4), (0, LANES - DIFF_DIM)))
    w_o = w_out.astype(BF16)
    w_up = ffn_w_up.astype(BF16)
    w_down = ffn_w_down.astype(BF16)
    conv_w = jnp.pad(f32(ffn_conv_w), ((0, 0), (0, SUBLANES - FFN_CONV), (0, 0)))
    for i in range(depth):
        inf, s_u, dqkv = _inproj(x2, attn_norm_w[i][None, :], w_gdn[i], w_s5[i], w_att[i], batch, seq, tm)
        o_gdn = _gdn(inf, gdn_conv_w[i], gate_p[i], f32(gdn_norm_w[i][None, :]), batch, seq, tl)
        o_s5 = _s5(s_u.reshape(batch, seq, S5_WIDTH), a_bar[i], bre[i], bim[i], cre[i], cim[i],
                   s5_d[i][None, :], w_glu[i], s5_b_glu[i][None, :], s5_norm_w[i][None, :], batch, seq, ts)
        lam_init = 0.8 - 0.6 * math.exp(-0.3 * i)
        o_diff = _diff_attn(dqkv, lam_p[i], diff_norm_w[i][None, :], batch, seq, tq, lam_init)
        x2 = _outproj(x2, o_gdn, o_s5.reshape(batch * seq, S5_WIDTH), o_diff, w_o[i], batch, seq, tm)
        x2 = _ffn(x2, ffn_norm_w[i][None, :], w_up[i], conv_w[i], ffn_conv_b[i][None, :], w_down[i],
                  final_norm_w[None, :], batch, seq, tm, final_norm=(i == depth - 1))
    return x2.reshape(batch, seq, D_MODEL)
```

```python
import functools
import math

import jax
import jax.numpy as jnp
from jax import lax
from jax.experimental import pallas as pl
from jax.experimental.pallas import tpu as pltpu

F32 = jnp.float32
BF16 = jnp.bfloat16
HIGHEST = lax.Precision.HIGHEST

D_MODEL = 1024
EPS = 1e-6
GDN_HEADS = 4
GDN_DIM = 128
GDN_WIDTH = GDN_HEADS * GDN_DIM
GDN_CONV = 4
GDN_CHUNK = 64
GDN_INV_BLOCK = 16
S5_GROUP = 16
S5_GROUPS = 32
S5_WIDTH = S5_GROUPS * S5_GROUP
S5_STATE = 64
S5_NSTATE = S5_GROUPS * S5_STATE
S5_HALF_IN = S5_WIDTH // 2
S5_HALF_ST = S5_NSTATE // 2
DIFF_HEADS = 4
DIFF_DIM = 64
DIFF_V = 2 * DIFF_DIM
DIFF_WIDTH = DIFF_HEADS * DIFF_V
MIX_WIDTH = GDN_WIDTH + S5_WIDTH + DIFF_WIDTH
D_FF = 2816
FFN_CONV = 3
FFN_CHUNK = 256
FFN_NCHUNK = D_FF // FFN_CHUNK
FFN_AHEAD = 2

LANES = 128
SUBLANES = 8
VMEM_LIMIT = 56 * 1024 * 1024

INF_QKV = 3 * GDN_WIDTH
INF_WIDTH = INF_QKV + GDN_WIDTH + LANES


def _dot(a, b):
    return jnp.dot(a, b, preferred_element_type=F32)


def _dot_nt(a, b, precision=None):
    return lax.dot_general(a, b, (((1,), (1,)), ((), ())), precision=precision,
                           preferred_element_type=F32)


def _dot_tn(a, b):
    return lax.dot_general(a, b, (((0,), (0,)), ((), ())), preferred_element_type=F32)


def _sigmoid(x):
    return 0.5 + 0.5 * jnp.tanh(0.5 * x)


def _silu(x):
    h = 0.5 * x
    return h + h * jnp.tanh(h)


def _params(sem):
    return pltpu.CompilerParams(dimension_semantics=sem, vmem_limit_bytes=VMEM_LIMIT)


def _inproj_kernel(x_ref, nw_ref, wf_ref, ws_ref, wa_ref, of_ref, os_ref, oa_ref):
    x = x_ref[...]
    ms = jnp.mean(x * x, axis=-1, keepdims=True)
    h = ((x * lax.rsqrt(ms + EPS)) * nw_ref[...]).astype(BF16)
    of_ref[...] = _dot(h, wf_ref[...])
    os_ref[...] = _dot(h, ws_ref[...])
    oa_ref[...] = _dot(h, wa_ref[...]).astype(BF16)


def _inproj(x2, nw, wf, ws, wa, batch, seq, tm):
    nt = seq // tm
    tokens = batch * seq
    row = lambda b, t: (b * nt + t, 0)
    const = lambda b, t: (0, 0)
    return pl.pallas_call(
        _inproj_kernel,
        grid=(batch, nt),
        in_specs=[pl.BlockSpec((tm, D_MODEL), row),
                  pl.BlockSpec((1, D_MODEL), const),
                  pl.BlockSpec((D_MODEL, INF_WIDTH), const),
                  pl.BlockSpec((D_MODEL, S5_WIDTH), const),
                  pl.BlockSpec((D_MODEL, 3 * DIFF_WIDTH), const)],
        out_specs=[pl.BlockSpec((tm, INF_WIDTH), row),
                   pl.BlockSpec((None, tm, S5_WIDTH), lambda b, t: (b, t, 0)),
                   pl.BlockSpec((tm, 3 * DIFF_WIDTH), row)],
        out_shape=[jax.ShapeDtypeStruct((tokens, INF_WIDTH), F32),
                   jax.ShapeDtypeStruct((batch, seq, S5_WIDTH), F32),
                   jax.ShapeDtypeStruct((tokens, 3 * DIFF_WIDTH), BF16)],
        compiler_params=_params(("parallel", "parallel")),
        name="inproj",
    )(x2, nw, wf, ws, wa)


def _unit_lower_inverse(mats, row, col):
    shift = int(math.log2(GDN_INV_BLOCK))
    same = (row >> shift) == (col >> shift)
    eye = (row == col).astype(F32)
    mm = lambda ps, qs: [_dot(p, q) for p, q in zip(ps, qs)]
    add = lambda ps, qs: [p + q for p, q in zip(ps, qs)]
    b16 = lambda ps: [p.astype(BF16) for p in ps]
    ad = [jnp.where(same, a, 0.0) for a in mats]
    ao = b16([jnp.where(same, 0.0, a) for a in mats])
    p = [eye - a for a in ad]
    x = b16(ad)
    for _ in range(shift - 1):
        x = b16(mm(x, x))
        p = add(p, mm(b16(p), x))
    td = b16(p)
    n1 = mm(td, ao)
    n1b = b16(n1)
    n2 = mm(n1b, n1b)
    n3 = mm(n1b, b16(n2))
    return mm(b16([eye - a + b - d for a, b, d in zip(n1, n2, n3)]), td)


def _gdn_kernel(qkv_ref, z_ref, ba_ref, cw_ref, gp_ref, nw_ref, o_ref, xs_ref, q_s, k_s, v_s, st_ref,
                *, tl):
    c = GDN_CHUNK
    ti = pl.program_id(1)

    @pl.when(ti == 0)
    def _():
        xs_ref[0:SUBLANES, :] = jnp.zeros((SUBLANES, INF_QKV), F32)
        st_ref[...] = jnp.zeros_like(st_ref)

    xs_ref[SUBLANES:SUBLANES + tl, :] = qkv_ref[...]

    def conv_head(h):
        for part, dst in enumerate((q_s, k_s, v_s)):
            lo = part * GDN_WIDTH + h * GDN_DIM
            acc = jnp.zeros((tl, GDN_DIM), F32)
            for j in range(GDN_CONV):
                off = SUBLANES - (GDN_CONV - 1) + j
                acc = acc + xs_ref[off:off + tl, lo:lo + GDN_DIM] * cw_ref[j:j + 1, lo:lo + GDN_DIM]
            y = _silu(acc)
            if part < 2:
                y = y * lax.rsqrt(jnp.sum(y * y, axis=-1, keepdims=True) + 1e-6)
            if part == 0:
                y = y * (GDN_DIM ** -0.5)
            dst[h] = y

    row = lax.broadcasted_iota(jnp.int32, (c, c), 0)
    col = lax.broadcasted_iota(jnp.int32, (c, c), 1)
    incl = row >= col
    strict = row > col
    tri = incl.astype(F32)
    sel_r = lax.broadcasted_iota(jnp.int32, (SUBLANES, LANES), 0)
    sel_c = lax.broadcasted_iota(jnp.int32, (SUBLANES, LANES), 1)
    sel = (sel_c == sel_r).astype(F32)

    nc = tl // c
    nh = GDN_HEADS

    def rows(ci):
        return slice(ci * c, (ci + 1) * c)

    def gates():
        ba = ba_ref[...]
        sp = ba + gp_ref[1:2, :]
        softplus = jnp.maximum(sp, 0.0) + math.log(2.0) * jnp.log2(
            1.0 + jnp.exp2(-math.log2(math.e) * jnp.abs(sp)))
        g_all = -jnp.exp(gp_ref[0:1, :]) * softplus
        beta_all = _sigmoid(ba)
        is_beta = lax.broadcasted_iota(jnp.int32, (c, LANES), 1) < nh
        grow, bgcol = [], []
        for ci in range(nc):
            gc_c = jnp.dot(tri, g_all[rows(ci), :], precision=HIGHEST, preferred_element_type=F32)
            bgcol.append(jnp.where(is_beta, beta_all[rows(ci), :], gc_c))
            grow.append(_dot_nt(sel, gc_c, precision=HIGHEST))
        return grow, bgcol

    def chunk_operators(units):
        q = [q_s[h, rows(ci), :] for ci, h in units]
        k = [k_s[h, rows(ci), :] for ci, h in units]
        v = [v_s[h, rows(ci), :] for ci, h in units]
        beta = [bgcol[ci][:, h:h + 1] for ci, h in units]
        gc = [bgcol[ci][:, nh + h:nh + h + 1] for ci, h in units]
        gr = [grow[ci][nh + h:nh + h + 1, :] for ci, h in units]
        decay = [jnp.where(incl, jnp.exp(jnp.where(incl, a - b, 0.0)), 0.0) for a, b in zip(gc, gr)]
        egc = [jnp.exp(a) for a in gc]
        kb = [a * b for a, b in zip(k, beta)]
        kbf = [a.astype(BF16) for a in k]
        kk = [_dot_nt(a.astype(BF16), b) for a, b in zip(kb, kbf)]
        qk = [_dot_nt(a.astype(BF16), b) for a, b in zip(q, kbf)]
        a_low = [jnp.where(strict, a * d, 0.0) for a, d in zip(kk, decay)]
        attn = [jnp.where(incl, a * d, 0.0).astype(BF16) for a, d in zip(qk, decay)]
        t_inv = [t.astype(BF16) for t in _unit_lower_inverse(a_low, row, col)]
        u_mat = [_dot(t, (a * b).astype(BF16)) for t, a, b in zip(t_inv, v, beta)]
        w_mat = [_dot(t, (a * e).astype(BF16)).astype(BF16) for t, a, e in zip(t_inv, kb, egc)]
        qg = [(a * e).astype(BF16) for a, e in zip(q, egc)]
        g_last = [a[c - 1:c, :] for a in gc]
        kd = [(a * jnp.exp(gl - g)).astype(BF16) for a, gl, g in zip(k, g_last, gc)]
        kd_u = [_dot_tn(a, b.astype(BF16)) for a, b in zip(kd, u_mat)]
        kd_w = [_dot_tn(a, b).astype(BF16) for a, b in zip(kd, w_mat)]
        return dict(u=u_mat, w=w_mat, qg=qg, attn=attn, kd_u=kd_u, kd_w=kd_w,
                    eg=[jnp.exp(gl) for gl in g_last])

    for h in range(nh):
        conv_head(h)
    xs_ref[0:SUBLANES, :] = xs_ref[tl:tl + SUBLANES, :]
    grow, bgcol = gates()
    units = [(ci, h) for ci in range(nc) for h in range(nh)]
    ops = chunk_operators(units)

    state = [st_ref[h] for h in range(nh)]
    s_in = []
    for ci in range(nc):
        s_in.append([s.astype(BF16) for s in state])
        us = [ci * nh + h for h in range(nh)]
        state = [state[h] * ops['eg'][u] + ops['kd_u'][u] - _dot(ops['kd_w'][u], s_in[ci][h])
                 for h, u in enumerate(us)]
    for h in range(nh):
        st_ref[h] = state[h]

    ws = [_dot(w, s_in[ci][h]) for (ci, h), w in zip(units, ops['w'])]
    qs = [_dot(g, s_in[ci][h]) for (ci, h), g in zip(units, ops['qg'])]
    v_new = [(a - b).astype(BF16) for a, b in zip(ops['u'], ws)]
    o = [a + _dot(t, b) for a, t, b in zip(qs, ops['attn'], v_new)]
    for (ci, h), oo in zip(units, o):
        on = oo * lax.rsqrt(jnp.mean(oo * oo, axis=-1, keepdims=True) + EPS) * nw_ref[...]
        zz = z_ref[rows(ci), h * GDN_DIM:(h + 1) * GDN_DIM]
        o_ref[rows(ci), h * GDN_DIM:(h + 1) * GDN_DIM] = (on * _silu(zz)).astype(BF16)


def _gdn(inf, conv_w, gate_p, norm_w, batch, seq, tl):
    nt = seq // tl
    tokens = batch * seq
    row = lambda b, t: (b * nt + t, 0)
    const = lambda b, t: (0, 0)
    return pl.pallas_call(
        functools.partial(_gdn_kernel, tl=tl),
        grid=(batch, nt),
        in_specs=[pl.BlockSpec((tl, INF_QKV), row),
                  pl.BlockSpec((tl, GDN_WIDTH), lambda b, t: (b * nt + t, INF_QKV // GDN_WIDTH)),
                  pl.BlockSpec((tl, LANES), lambda b, t: (b * nt + t, (INF_QKV + GDN_WIDTH) // LANES)),
                  pl.BlockSpec((GDN_CONV, INF_QKV), const),
                  pl.BlockSpec((SUBLANES, LANES), const),
                  pl.BlockSpec((1, GDN_DIM), const)],
        out_specs=pl.BlockSpec((tl, GDN_WIDTH), row),
        out_shape=jax.ShapeDtypeStruct((tokens, GDN_WIDTH), BF16),
        scratch_shapes=[pltpu.VMEM((tl + SUBLANES, INF_QKV), F32),
                        pltpu.VMEM((GDN_HEADS, tl, GDN_DIM), F32),
                        pltpu.VMEM((GDN_HEADS, tl, GDN_DIM), F32),
                        pltpu.VMEM((GDN_HEADS, tl, GDN_DIM), F32),
                        pltpu.VMEM((GDN_HEADS, GDN_DIM, GDN_DIM), F32)],
        compiler_params=_params(("parallel", "arbitrary")),
        name="gdn",
    )(inf, inf, inf, conv_w, gate_p, norm_w)


S5_SCAN_LANES = 512


def _gelu_tanh(x):
    return 0.5 * x * (1.0 + jnp.tanh(math.sqrt(2.0 / math.pi) * (x + 0.044715 * (x * x * x))))


def _s5_kernel(u_ref, a_ref, bre_ref, bim_ref, cre_ref, cim_ref, d_ref, wg_ref, bg_ref, nw_ref, o_ref,
               sre_ref, sim_ref, *, batch, tl):
    rows = batch * tl

    @pl.when(pl.program_id(0) == 0)
    def _():
        sre_ref[0:batch, :] = jnp.zeros((batch, S5_NSTATE), F32)
        sim_ref[0:batch, :] = jnp.zeros((batch, S5_NSTATE), F32)

    u_bm = u_ref[...].reshape(rows, S5_WIDTH)
    r_i = lax.broadcasted_iota(jnp.int32, (rows, rows), 0)
    c_i = lax.broadcasted_iota(jnp.int32, (rows, rows), 1)
    to_tm = (c_i == (r_i % batch) * tl + r_i // batch).astype(BF16)
    to_bm = (c_i == (r_i % tl) * batch + r_i // tl).astype(BF16)
    u_hi = u_bm.astype(BF16)
    u_lo = (u_bm - u_hi.astype(F32)).astype(BF16)
    u_hi_tm = _dot(to_tm, u_hi)
    u = u_hi_tm + _dot(to_tm, u_lo)
    ub = u_hi_tm.astype(BF16)
    for hf in range(2):
        cin = slice(hf * S5_HALF_IN, (hf + 1) * S5_HALF_IN)
        cst = slice(hf * S5_HALF_ST, (hf + 1) * S5_HALF_ST)
        sre_ref[batch:batch + rows, cst] = _dot(ub[:, cin], bre_ref[hf])
        sim_ref[batch:batch + rows, cst] = _dot(ub[:, cin], bim_ref[hf])

    def scan(lc):
        cols = slice(lc * S5_SCAN_LANES, (lc + 1) * S5_SCAN_LANES)
        a_re = jnp.broadcast_to(a_ref[0:1, cols], (batch, S5_SCAN_LANES))
        a_im = jnp.broadcast_to(a_ref[1:2, cols], (batch, S5_SCAN_LANES))
        s_re, s_im = sre_ref[0:batch, cols], sim_ref[0:batch, cols]
        for t in range(tl):
            r = slice((t + 1) * batch, (t + 2) * batch)
            s_re, s_im = (a_re * s_re - a_im * s_im + sre_ref[r, cols],
                          a_re * s_im + a_im * s_re + sim_ref[r, cols])
            sre_ref[r, cols] = s_re
            sim_ref[r, cols] = s_im
        sre_ref[0:batch, cols] = s_re
        sim_ref[0:batch, cols] = s_im

    ys = []
    chunks_per_half = S5_HALF_ST // S5_SCAN_LANES
    for hf in range(2):
        for lc in range(hf * chunks_per_half, (hf + 1) * chunks_per_half):
            scan(lc)
        cst = slice(hf * S5_HALF_ST, (hf + 1) * S5_HALF_ST)
        s_re = sre_ref[batch:batch + rows, cst].astype(BF16)
        s_im = sim_ref[batch:batch + rows, cst].astype(BF16)
        ys.append(_dot(s_re, cre_ref[hf]) - _dot(s_im, cim_ref[hf]))
    y = jnp.concatenate(ys, axis=-1) + d_ref[...] * u
    y = _gelu_tanh(y)
    y = y * _sigmoid(_dot(y.astype(BF16), wg_ref[...]) + bg_ref[...])
    y = (y * lax.rsqrt(jnp.mean(y * y, axis=-1, keepdims=True) + EPS) * nw_ref[...]).astype(BF16)
    o_ref[...] = _dot(to_bm, y).astype(BF16).reshape(batch, tl, S5_WIDTH)


def _s5(u3, a_bar, bre, bim, cre, cim, d_skip, w_glu, b_glu, norm_w, batch, seq, tl):
    rows = batch * tl
    blk = lambda t: (0, t, 0)
    c2 = lambda t: (0, 0)
    c3 = lambda t: (0, 0, 0)
    return pl.pallas_call(
        functools.partial(_s5_kernel, batch=batch, tl=tl),
        grid=(seq // tl,),
        in_specs=[pl.BlockSpec((batch, tl, S5_WIDTH), blk),
                  pl.BlockSpec((SUBLANES, S5_NSTATE), c2),
                  pl.BlockSpec((2, S5_HALF_IN, S5_HALF_ST), c3),
                  pl.BlockSpec((2, S5_HALF_IN, S5_HALF_ST), c3),
                  pl.BlockSpec((2, S5_HALF_ST, S5_HALF_IN), c3),
                  pl.BlockSpec((2, S5_HALF_ST, S5_HALF_IN), c3),
                  pl.BlockSpec((1, S5_WIDTH), c2),
                  pl.BlockSpec((S5_WIDTH, S5_WIDTH), c2),
                  pl.BlockSpec((1, S5_WIDTH), c2),
                  pl.BlockSpec((1, S5_WIDTH), c2)],
        out_specs=pl.BlockSpec((batch, tl, S5_WIDTH), blk),
        out_shape=jax.ShapeDtypeStruct((batch, seq, S5_WIDTH), BF16),
        scratch_shapes=[pltpu.VMEM((batch + rows, S5_NSTATE), F32),
                        pltpu.VMEM((batch + rows, S5_NSTATE), F32)],
        compiler_params=_params(("arbitrary",)),
        name="s5",
    )(u3, a_bar, bre, bim, cre, cim, d_skip, w_glu, b_glu, norm_w)


def _s5_operators(lam_re, lam_im, log_dt, b_re, b_im, c_re, c_im):
    depth = lam_re.shape[0]
    l_re = jnp.minimum(lam_re.astype(F32), -1e-4)
    l_im = lam_im.astype(F32)
    dt = jnp.exp(log_dt.astype(F32))[..., None]
    mag = jnp.exp(l_re * dt)
    a_re = mag * jnp.cos(l_im * dt)
    a_im = mag * jnp.sin(l_im * dt)
    den = l_re * l_re + l_im * l_im
    f_re = ((a_re - 1.0) * l_re + a_im * l_im) / den
    f_im = (a_im * l_re - (a_re - 1.0) * l_im) / den
    bb_re = f_re[..., None] * b_re.astype(F32) - f_im[..., None] * b_im.astype(F32)
    bb_im = f_re[..., None] * b_im.astype(F32) + f_im[..., None] * b_re.astype(F32)
    a_bar = jnp.pad(jnp.stack([a_re.reshape(depth, -1), a_im.reshape(depth, -1)], axis=1),
                    ((0, 0), (0, SUBLANES - 2), (0, 0)))
    eye = jnp.eye(S5_GROUPS, dtype=F32)

    def b_op(t):
        full = jnp.einsum('dgph,gk->dghkp', t, eye).reshape(depth, S5_WIDTH, S5_NSTATE)
        return jnp.stack([full[:, :S5_HALF_IN, :S5_HALF_ST], full[:, S5_HALF_IN:, S5_HALF_ST:]],
                         axis=1).astype(BF16)

    def c_op(t):
        full = jnp.einsum('dghp,gk->dgpkh', t, eye).reshape(depth, S5_NSTATE, S5_WIDTH)
        return jnp.stack([full[:, :S5_HALF_ST, :S5_HALF_IN], full[:, S5_HALF_ST:, S5_HALF_IN:]],
                         axis=1).astype(BF16)

    return a_bar, b_op(bb_re), b_op(bb_im), c_op(c_re.astype(F32)), c_op(c_im.astype(F32))


ATT_ROWS = 32


def _diff_kernel(q_ref, k_ref, v_ref, lp_ref, nw_ref, o_ref, m_ref, l_ref, acc_ref, *, tq, lam_init):
    qi = pl.program_id(1)
    nh = DIFF_HEADS
    lane = lax.broadcasted_iota(jnp.int32, (tq, DIFF_V), 1)
    scale = DIFF_DIM ** -0.5 * math.log2(math.e)
    qs = []
    for h in range(nh):
        q = q_ref[:, h * DIFF_V:(h + 1) * DIFF_V]
        zero = jnp.zeros_like(q)
        q0 = jnp.where(lane < DIFF_DIM, q, zero)
        q1 = jnp.where(lane < DIFF_DIM, zero, q)
        qs.append((jnp.concatenate([q0, q1], axis=0).astype(F32) * scale).astype(BF16))

    m_ref[...] = jnp.full_like(m_ref, -jnp.inf)
    l_ref[...] = jnp.zeros_like(l_ref)
    acc_ref[...] = jnp.zeros_like(acc_ref)

    def update(kb, masked, nk=1):
        tk = nk * tq
        r = pl.ds(pl.multiple_of(kb * tq, tq), tk)
        hs = range(nh)
        kk = [k_ref[r, h * DIFF_V:(h + 1) * DIFF_V] for h in hs]
        vv = [v_ref[r, h * DIFF_V:(h + 1) * DIFF_V] for h in hs]
        s = {0: _dot_nt(qs[0], kk[0])}
        for h in hs:
            if h + 1 < nh:
                s[h + 1] = _dot_nt(qs[h + 1], kk[h + 1])
            pg, ag = [], []
            for g in range(2 * tq // ATT_ROWS):
                r0 = g * ATT_ROWS
                sg = s[h][r0:r0 + ATT_ROWS, :]
                if masked:
                    rr = lax.broadcasted_iota(jnp.int32, (ATT_ROWS, tk), 0) + (r0 % tq)
                    cc = lax.broadcasted_iota(jnp.int32, (ATT_ROWS, tk), 1) - (nk - 1) * tq
                    sg = jnp.where(rr >= cc, sg, -jnp.inf)
                tiles = [sg[:, j * LANES:(j + 1) * LANES] for j in range(tk // LANES)]
                m_cur = jnp.max(functools.reduce(jnp.maximum, tiles), axis=-1, keepdims=True)
                m_prev = m_ref[h, r0:r0 + ATT_ROWS, :]
                m_new = jnp.maximum(m_prev, m_cur)
                a = jnp.exp2(m_prev - m_new)
                p = [jnp.exp2(t - m_new) for t in tiles]
                l_ref[h, r0:r0 + ATT_ROWS, :] = a * l_ref[h, r0:r0 + ATT_ROWS, :] + functools.reduce(jnp.add, p)
                m_ref[h, r0:r0 + ATT_ROWS, :] = m_new
                pg.append(jnp.concatenate([t.astype(BF16) for t in p], axis=-1))
                ag.append(a)
            pv = _dot(jnp.concatenate(pg, axis=0), vv[h])
            acc_ref[h] = jnp.concatenate(ag, axis=0) * acc_ref[h] + pv

    def pair(i, carry):
        update(2 * i, False, nk=2)
        return carry

    lax.fori_loop(0, qi // 2, pair, 0)

    @pl.when(qi % 2 == 1)
    def _():
        update(qi - 1, False)

    update(qi, True)

    lp = lp_ref[...]
    lam = (jnp.exp(jnp.sum(lp[0:1, :] * lp[1:2, :], axis=-1, keepdims=True))
           - jnp.exp(jnp.sum(lp[2:3, :] * lp[3:4, :], axis=-1, keepdims=True)) + lam_init)
    for h in range(nh):
        o = acc_ref[h] / jnp.sum(l_ref[h], axis=-1, keepdims=True)
        o = o[:tq, :] - lam * o[tq:, :]
        on = o * lax.rsqrt(jnp.mean(o * o, axis=-1, keepdims=True) + EPS) * nw_ref[...]
        o_ref[:, h * DIFF_V:(h + 1) * DIFF_V] = (on * (1.0 - lam_init)).astype(BF16)


def _diff_attn(qkv, lam_p, norm_w, batch, seq, tq, lam_init):
    nq = seq // tq
    tokens = batch * seq
    return pl.pallas_call(
        functools.partial(_diff_kernel, tq=tq, lam_init=lam_init),
        grid=(batch, nq),
        in_specs=[pl.BlockSpec((tq, DIFF_WIDTH), lambda b, i: (b * nq + i, 0)),
                  pl.BlockSpec((seq, DIFF_WIDTH), lambda b, i: (b, 1)),
                  pl.BlockSpec((seq, DIFF_WIDTH), lambda b, i: (b, 2)),
                  pl.BlockSpec((SUBLANES, LANES), lambda b, i: (0, 0)),
                  pl.BlockSpec((1, DIFF_V), lambda b, i: (0, 0))],
        out_specs=pl.BlockSpec((tq, DIFF_WIDTH), lambda b, i: (b * nq + i, 0)),
        out_shape=jax.ShapeDtypeStruct((tokens, DIFF_WIDTH), BF16),
        scratch_shapes=[pltpu.VMEM((DIFF_HEADS, 2 * tq, LANES), F32),
                        pltpu.VMEM((DIFF_HEADS, 2 * tq, LANES), F32),
                        pltpu.VMEM((DIFF_HEADS, 2 * tq, DIFF_V), F32)],
        compiler_params=_params(("parallel", "parallel")),
        name="diff_attn",
    )(qkv, qkv, qkv, lam_p, norm_w)


def _outproj_kernel(x_ref, og_ref, os_ref, od_ref, w_ref, o_ref):
    acc = _dot(og_ref[...], w_ref[0:GDN_WIDTH, :])
    acc = acc + _dot(os_ref[...], w_ref[GDN_WIDTH:GDN_WIDTH + S5_WIDTH, :])
    acc = acc + _dot(od_ref[...], w_ref[GDN_WIDTH + S5_WIDTH:MIX_WIDTH, :])
    o_ref[...] = x_ref[...] + acc


def _outproj(x2, o_gdn, o_s5, o_diff, w_out, batch, seq, tm):
    nt = seq // tm
    row = lambda b, t: (b * nt + t, 0)
    return pl.pallas_call(
        _outproj_kernel,
        grid=(batch, nt),
        in_specs=[pl.BlockSpec((tm, D_MODEL), row),
                  pl.BlockSpec((tm, GDN_WIDTH), row),
                  pl.BlockSpec((None, tm, S5_WIDTH), lambda b, t: (b, t, 0)),
                  pl.BlockSpec((tm, DIFF_WIDTH), row),
                  pl.BlockSpec((MIX_WIDTH, D_MODEL), lambda b, t: (0, 0))],
        out_specs=pl.BlockSpec((tm, D_MODEL), row),
        out_shape=jax.ShapeDtypeStruct(x2.shape, F32),
        compiler_params=_params(("parallel", "parallel")),
        name="outproj",
    )(x2, o_gdn, o_s5, o_diff, w_out)


def _ffn_kernel(x_ref, nw_ref, wu_ref, cw_ref, cb_ref, wd_ref, fw_ref, o_ref, ub_ref, tail_ref, act_ref,
                *, tm, final_norm):
    @pl.when(pl.program_id(1) == 0)
    def _():
        tail_ref[...] = jnp.zeros_like(tail_ref)

    x = x_ref[...]
    ms = jnp.mean(x * x, axis=-1, keepdims=True)
    h = ((x * lax.rsqrt(ms + EPS)) * nw_ref[...]).astype(BF16)
    nslot = FFN_AHEAD + 1
    def cols(j):
        return [slice(part * D_FF + j * FFN_CHUNK, part * D_FF + (j + 1) * FFN_CHUNK) for part in range(2)]

    def up(j):
        for part, cs in enumerate(cols(j)):
            ub_ref[j % nslot, SUBLANES:SUBLANES + tm, part * FFN_CHUNK:(part + 1) * FFN_CHUNK] = (
                _dot(h, wu_ref[:, cs]))

    for j in range(min(FFN_AHEAD, FFN_NCHUNK)):
        up(j)
    for j in range(FFN_NCHUNK):
        slot = j % nslot
        if j + FFN_AHEAD < FFN_NCHUNK:
            up(j + FFN_AHEAD)
        ub_ref[slot, 0:SUBLANES, :] = tail_ref[j]
        tail_ref[j] = ub_ref[slot, tm:tm + SUBLANES, :]
        cv = []
        for part, cs in enumerate(cols(j)):
            acc = cb_ref[:, cs]
            for t in range(FFN_CONV):
                off = SUBLANES - (FFN_CONV - 1) + t
                acc = acc + (ub_ref[slot, off:off + tm, part * FFN_CHUNK:(part + 1) * FFN_CHUNK]
                             * cw_ref[t:t + 1, cs])
            cv.append(acc)
        act = _silu(cv[0]) * cv[1]
        act_ref[:, j * FFN_CHUNK:(j + 1) * FFN_CHUNK] = act.astype(BF16)
    y = x + _dot(act_ref[...], wd_ref[...])
    if final_norm:
        y = y * lax.rsqrt(jnp.mean(y * y, axis=-1, keepdims=True) + EPS) * fw_ref[...]
    o_ref[...] = y


def _ffn(x2, norm_w, w_up, conv_w, conv_b, w_down, final_w, batch, seq, tm, final_norm):
    nt = seq // tm
    row = lambda b, t: (b * nt + t, 0)
    c2 = lambda b, t: (0, 0)
    c3 = lambda b, t: (0, 0, 0)
    once = pl.Buffered(1)
    return pl.pallas_call(
        functools.partial(_ffn_kernel, tm=tm, final_norm=final_norm),
        grid=(batch, nt),
        in_specs=[pl.BlockSpec((tm, D_MODEL), row),
                  pl.BlockSpec((1, D_MODEL), c2),
                  pl.BlockSpec((D_MODEL, 2 * D_FF), c2, pipeline_mode=once),
                  pl.BlockSpec((SUBLANES, 2 * D_FF), c2),
                  pl.BlockSpec((1, 2 * D_FF), c2),
                  pl.BlockSpec((D_FF, D_MODEL), c2, pipeline_mode=once),
                  pl.BlockSpec((1, D_MODEL), c2)],
        out_specs=pl.BlockSpec((tm, D_MODEL), row),
        out_shape=jax.ShapeDtypeStruct(x2.shape, F32),
        scratch_shapes=[pltpu.VMEM((FFN_AHEAD + 1, tm + SUBLANES, 2 * FFN_CHUNK), F32),
                        pltpu.VMEM((FFN_NCHUNK, SUBLANES, 2 * FFN_CHUNK), F32),
                        pltpu.VMEM((tm, D_FF), BF16)],
        compiler_params=_params(("parallel", "arbitrary")),
        name="ffn",
    )(x2, norm_w, w_up, conv_w, conv_b, w_down, final_w)


def kernel(x, attn_norm_w, w_in, gdn_conv_w, gdn_a_log, gdn_dt_bias, gdn_norm_w, s5_lambda_re, s5_lambda_im, s5_log_dt, s5_b_re, s5_b_im, s5_c_re, s5_c_im, s5_d, s5_w_glu, s5_b_glu, s5_norm_w, diff_lambda_q1, diff_lambda_k1, diff_lambda_q2, diff_lambda_k2, diff_norm_w, w_out, ffn_norm_w, ffn_w_up, ffn_conv_w, ffn_conv_b, ffn_w_down, final_norm_w):
    batch, seq, _ = x.shape
    depth = w_in.shape[0]
    tm = min(1024, seq)
    tl = min(512, seq)
    tq = min(512, seq)
    ts = min(32, seq)
    assert seq % tm == 0 and seq % tl == 0 and seq % tq == 0 and seq % ts == 0
    assert tl % GDN_CHUNK == 0 and batch % SUBLANES == 0

    x2 = x.reshape(batch * seq, D_MODEL).astype(F32)
    o_b, o_su = 4 * GDN_WIDTH, 4 * GDN_WIDTH + 2 * GDN_HEADS
    o_dq = o_su + S5_WIDTH
    f32 = lambda t: t.astype(F32)
    w_gdn = jnp.pad(w_in[:, :, :o_su], ((0, 0), (0, 0), (0, LANES - 2 * GDN_HEADS))).astype(BF16)
    w_s5 = w_in[:, :, o_su:o_dq].astype(BF16)
    w_att = w_in[:, :, o_dq:].astype(BF16)
    gate_p = jnp.pad(jnp.stack([f32(gdn_a_log), f32(gdn_dt_bias)], axis=1),
                     ((0, 0), (0, SUBLANES - 2), (GDN_HEADS, LANES - 2 * GDN_HEADS)))
    a_bar, bre, bim, cre, cim = _s5_operators(s5_lambda_re, s5_lambda_im, s5_log_dt,
                                              s5_b_re, s5_b_im, s5_c_re, s5_c_im)
    w_glu = s5_w_glu.astype(BF16)
    lam_p = jnp.pad(jnp.stack([f32(diff_lambda_q1), f32(diff_lambda_k1), f32(diff_lambda_q2),
                               f32(diff_lambda_k2)], axis=1),
                    ((0, 0), (0, SUBLANES - 4), (0, LANES - DIFF_DIM)))
    w_o = w_out.astype(BF16)
    w_up = ffn_w_up.astype(BF16)
    w_down = ffn_w_down.astype(BF16)
    conv_w = jnp.pad(f32(ffn_conv_w), ((0, 0), (0, SUBLANES - FFN_CONV), (0, 0)))
    for i in range(depth):
        inf, s_u, dqkv = _inproj(x2, attn_norm_w[i][None, :], w_gdn[i], w_s5[i], w_att[i], batch, seq, tm)
        o_gdn = _gdn(inf, gdn_conv_w[i], gate_p[i], f32(gdn_norm_w[i][None, :]), batch, seq, tl)
        o_s5 = _s5(s_u, a_bar[i], bre[i], bim[i], cre[i], cim[i],
                   s5_d[i][None, :], w_glu[i], s5_b_glu[i][None, :], s5_norm_w[i][None, :], batch, seq, ts)
        lam_init = 0.8 - 0.6 * math.exp(-0.3 * i)
        o_diff = _diff_attn(dqkv, lam_p[i], diff_norm_w[i][None, :], batch, seq, tq, lam_init)
        x2 = _outproj(x2, o_gdn, o_s5, o_diff, w_o[i], batch, seq, tm)
        x2 = _ffn(x2, ffn_norm_w[i][None, :], w_up[i], conv_w[i], ffn_conv_b[i][None, :], w_down[i],
                  final_norm_w[None, :], batch, seq, tm, final_norm=(i == depth - 1))
    return x2.reshape(batch, seq, D_MODEL)
```
